```python
import jax, jax.numpy as jnp
from jax import lax
import numpy as np

D_MODEL = 2048
BATCH = 4
SEQ = 2048
DEPTH = 4
DEC_BATCH = 128
DEC_SEQ = 1
PAST_LEN = 16384
PAGE_SIZE = 128

HEAD_DIM = 128
POOL_WIDTH = D_MODEL // 4
POOL_GROUPS = 4
POOL_GROUP_DIM = POOL_WIDTH // POOL_GROUPS
POOL_WINDOWS = (2, 4, 8, 16)
POOL_STATE = max(POOL_WINDOWS) - 1
SCONV_WIDTH = 3 * D_MODEL // 8
SCONV_K = 3
CCONV_WIDTH = D_MODEL - POOL_WIDTH - SCONV_WIDTH
CCONV_K = 31
D_IN = POOL_WIDTH + 3 * SCONV_WIDTH + 2 * CCONV_WIDTH
D_FF = -(-8 * D_MODEL // (3 * 256)) * 256
EPS = 1e-6

kernel_name = "hybrid_pool_shortconv_conformer_decode_step"


def _rmsnorm(x, g):
    xf = x.astype(jnp.float32)
    y = xf * lax.rsqrt(jnp.mean(xf * xf, axis=-1, keepdims=True) + EPS)
    return (y * g.astype(jnp.float32)).astype(x.dtype)


def _layernorm(x, g, b):
    xf = x.astype(jnp.float32)
    mu = jnp.mean(xf, axis=-1, keepdims=True)
    var = jnp.mean(jnp.square(xf - mu), axis=-1, keepdims=True)
    y = (xf - mu) * lax.rsqrt(var + EPS)
    return (y * g.astype(jnp.float32) + b.astype(jnp.float32)).astype(x.dtype)


def _causal_dwconv(h, state, w):
    k = w.shape[0]
    ext = jnp.concatenate([state.astype(h.dtype), h], axis=1)
    y = lax.conv_general_dilated(
        ext, w[:, None, :].astype(h.dtype), window_strides=(1,), padding="VALID",
        dimension_numbers=("NWC", "WIO", "NWC"), feature_group_count=h.shape[-1])
    return y, ext[:, ext.shape[1] - (k - 1):, :]


def _pool_mix(u, state, pos, w_grp, scale):
    n, t, _ = u.shape
    ext = jnp.concatenate([state.astype(u.dtype), u], axis=1)
    extf = ext.astype(jnp.float32)
    cs = jnp.concatenate([jnp.zeros_like(extf[:, :1]), jnp.cumsum(extf, axis=1)], axis=1)
    end = cs[:, POOL_STATE + 1:, :]
    outs = []
    for g, win in enumerate(POOL_WINDOWS):
        lo, hi = g * POOL_GROUP_DIM, (g + 1) * POOL_GROUP_DIM
        start = cs[:, POOL_STATE + 1 - win: POOL_STATE + 1 - win + t, lo:hi]
        cnt = jnp.minimum(pos + 1, win).astype(jnp.float32)[None, :, None]
        outs.append((end[..., lo:hi] - start) / cnt)
    pooled = (jnp.concatenate(outs, axis=-1) - extf[:, POOL_STATE:, :]).astype(u.dtype)
    pooled = pooled.reshape(n, t, POOL_GROUPS, POOL_GROUP_DIM)
    mixed = jnp.einsum("btgc,gcd->btgd", pooled, w_grp).reshape(n, t, POOL_WIDTH)
    return mixed * scale, ext[:, ext.shape[1] - POOL_STATE:, :]


def _layer(x, pos, st_pool, st_sconv, st_cconv,
           norm_mix, w_in, pool_w, pool_scale, sconv_w, cconv_w, cconv_b,
           cconv_ln_g, cconv_ln_b, w_out, norm_ffn, w_gate, w_up, w_down):
    h = _rmsnorm(x, norm_mix)
    z = jnp.einsum("btd,de->bte", h, w_in)
    o1 = POOL_WIDTH
    o2 = o1 + 3 * SCONV_WIDTH
    u_a = z[..., :o1]
    b_gate, c_gate, x_b = jnp.split(z[..., o1:o2], 3, axis=-1)
    a_c, g_c = jnp.split(z[..., o2:], 2, axis=-1)
    y_a, ns_pool = _pool_mix(u_a, st_pool, pos, pool_w, pool_scale)
    conv_b, ns_sconv = _causal_dwconv(c_gate * x_b, st_sconv, sconv_w)
    y_b = b_gate * conv_b
    glu = a_c * jax.nn.sigmoid(g_c)
    conv_c, ns_cconv = _causal_dwconv(glu, st_cconv, cconv_w)
    y_c = jax.nn.silu(_layernorm(conv_c + cconv_b, cconv_ln_g, cconv_ln_b))
    mix = jnp.concatenate([y_a, y_b, y_c], axis=-1)
    x = x + jnp.einsum("btd,de->bte", mix, w_out)
    h2 = _rmsnorm(x, norm_ffn)
    f = jax.nn.silu(jnp.einsum("btd,df->btf", h2, w_gate)) * jnp.einsum("btd,df->btf", h2, w_up)
    x = x + jnp.einsum("btf,fd->btd", f, w_down)
    return x, ns_pool, ns_sconv, ns_cconv


def _trunk(x, pos, st_pool, st_sconv, st_cconv, layer_params, norm_final):
    sp, ss, sc = [], [], []
    for l in range(DEPTH):
        x, p, s, c = _layer(x, pos, st_pool[l], st_sconv[l], st_cconv[l],
                            *[w[l] for w in layer_params])
        sp.append(p); ss.append(s); sc.append(c)
    return _rmsnorm(x, norm_final), jnp.stack(sp), jnp.stack(ss), jnp.stack(sc)


def setup_inputs(seed: int = 0) -> dict:
    key = jax.random.key(seed)
    ks = jax.random.split(key, 24)
    f32 = jnp.float32
    nrm = lambda k, s, sc: jax.random.normal(k, s, f32) * sc
    return {
        "x_prompt": nrm(ks[0], (BATCH, SEQ, D_MODEL), 1.0),
        "x_sample": nrm(ks[1], (DEC_BATCH, DEC_SEQ, D_MODEL), 1.0),
        "state_pool": nrm(ks[2], (DEPTH, DEC_BATCH, POOL_STATE, POOL_WIDTH), 1.0),
        "state_sconv": nrm(ks[3], (DEPTH, DEC_BATCH, SCONV_K - 1, SCONV_WIDTH), 1.0),
        "state_cconv": nrm(ks[4], (DEPTH, DEC_BATCH, CCONV_K - 1, CCONV_WIDTH), 1.0),
        "norm_mix": 1.0 + nrm(ks[5], (DEPTH, D_MODEL), 0.02),
        "w_in": nrm(ks[6], (DEPTH, D_MODEL, D_IN), D_MODEL ** -0.5),
        "pool_w": nrm(ks[7], (DEPTH, POOL_GROUPS, POOL_GROUP_DIM, POOL_GROUP_DIM), POOL_GROUP_DIM ** -0.5),
        "pool_scale": 1.0 + nrm(ks[8], (DEPTH, POOL_WIDTH), 0.02),
        "sconv_w": nrm(ks[9], (DEPTH, SCONV_K, SCONV_WIDTH), SCONV_K ** -0.5),
        "cconv_w": nrm(ks[10], (DEPTH, CCONV_K, CCONV_WIDTH), CCONV_K ** -0.5),
        "cconv_b": nrm(ks[11], (DEPTH, CCONV_WIDTH), 0.01),
        "cconv_ln_g": 1.0 + nrm(ks[12], (DEPTH, CCONV_WIDTH), 0.02),
        "cconv_ln_b": nrm(ks[13], (DEPTH, CCONV_WIDTH), 0.01),
        "w_out": nrm(ks[14], (DEPTH, D_MODEL, D_MODEL), D_MODEL ** -0.5),
        "norm_ffn": 1.0 + nrm(ks[15], (DEPTH, D_MODEL), 0.02),
        "w_gate": nrm(ks[16], (DEPTH, D_MODEL, D_FF), D_MODEL ** -0.5),
        "w_up": nrm(ks[17], (DEPTH, D_MODEL, D_FF), D_MODEL ** -0.5),
        "w_down": nrm(ks[18], (DEPTH, D_FF, D_MODEL), D_FF ** -0.5),
        "norm_final": 1.0 + nrm(ks[19], (D_MODEL,), 0.02),
    }


def reference(x_prompt, x_sample, state_pool, state_sconv, state_cconv,
              norm_mix, w_in, pool_w, pool_scale, sconv_w, cconv_w, cconv_b,
              cconv_ln_g, cconv_ln_b, w_out, norm_ffn, w_gate, w_up, w_down, norm_final):
    layer_params = (norm_mix, w_in, pool_w, pool_scale, sconv_w, cconv_w, cconv_b,
                    cconv_ln_g, cconv_ln_b, w_out, norm_ffn, w_gate, w_up, w_down)
    dt = x_prompt.dtype
    nb = x_prompt.shape[0]
    z_pool = jnp.zeros((DEPTH, nb, POOL_STATE, POOL_WIDTH), dt)
    z_sconv = jnp.zeros((DEPTH, nb, SCONV_K - 1, SCONV_WIDTH), dt)
    z_cconv = jnp.zeros((DEPTH, nb, CCONV_K - 1, CCONV_WIDTH), dt)
    pos_prompt = jnp.arange(x_prompt.shape[1], dtype=jnp.int32)
    y_prompt, sp_p, ss_p, sc_p = _trunk(x_prompt, pos_prompt, z_pool, z_sconv, z_cconv,
                                        layer_params, norm_final)
    pos_sample = PAST_LEN + jnp.arange(x_sample.shape[1], dtype=jnp.int32)
    y_sample, sp_s, ss_s, sc_s = _trunk(x_sample, pos_sample, state_pool, state_sconv, state_cconv,
                                        layer_params, norm_final)
    return (y_prompt, y_sample, sp_p, sp_s, ss_p, ss_s, sc_p, sc_s)
```

```python
import functools

import jax
import jax.numpy as jnp
from jax import lax
from jax.experimental import pallas as pl
from jax.experimental.pallas import tpu as pltpu

EPS = 1e-6
PAST_LEN = 16384
POOL_WINDOWS = (2, 4, 8, 16)
POOL_GROUP_DIM = 128
POOL_STATE = 15
SCONV_K = 3
CCONV_K = 31

SUBLANES = 8
LANES = 128
HIST_A = 16
HIST_B = 8
HIST_C = 32
VMEM_LIMIT_BYTES = 58 * 1024 * 1024

BF16 = jnp.bfloat16
F32 = jnp.float32


def _tiles(n_rows):
    mixer_tm = min(256, n_rows)
    ffn_tm = min(1024, n_rows)
    ffn_tf = 512
    return mixer_tm, ffn_tm, ffn_tf


def _rmsnorm(x, g):
    ms = jnp.mean(x * x, axis=-1, keepdims=True)
    return x * lax.rsqrt(ms + EPS) * g


def _dot(a, b):
    return jnp.dot(a, b, preferred_element_type=F32)


def _sigmoid(x):
    return 1.0 / (1.0 + jnp.exp(-x))


def _layernorm_silu(c, g, b):
    mu = jnp.mean(c, axis=-1, keepdims=True)
    d = c - mu
    var = jnp.mean(d * d, axis=-1, keepdims=True)
    y = d * lax.rsqrt(var + EPS) * g + b
    return y * _sigmoid(y)


def _in_proj_sections(p_width, s_width, c_width):
    o = [0, p_width]
    for w in (s_width, s_width, s_width, c_width, c_width):
        o.append(o[-1] + w)
    names = ("u_a", "b_gate", "c_gate", "x_b", "a_c", "g_c")
    return {n: (o[i], o[i + 1]) for i, n in enumerate(names)}


def _prompt_mixer_kernel(x_ref, nm_ref, win_ref, pw_ref, ps_ref, sw_ref, cw_ref,
                         cb_ref, lg_ref, lb_ref, wout_ref,
                         xo_ref, nsp_ref, nss_ref, nsc_ref,
                         ea_ref, eb_ref, ec_ref, xs_ref, bg_ref, cc_ref, mix_ref,
                         *, tm, sec):
    t = pl.program_id(1)
    p_width = ea_ref.shape[1]
    s_width = eb_ref.shape[1]
    c_width = ec_ref.shape[1]

    @pl.when(t == 0)
    def _():
        ea_ref[0:HIST_A, :] = jnp.zeros((HIST_A, p_width), F32)
        eb_ref[0:HIST_B, :] = jnp.zeros((HIST_B, s_width), F32)
        ec_ref[0:HIST_C, :] = jnp.zeros((HIST_C, c_width), F32)

    x = x_ref[0]
    h = _rmsnorm(x, nm_ref[...]).astype(BF16)

    def proj(name):
        lo, hi = sec[name]
        return _dot(h, win_ref[:, lo:hi])

    ea_ref[HIST_A:HIST_A + tm, :] = proj("u_a")
    bg_ref[...] = proj("b_gate")
    eb_ref[HIST_B:HIST_B + tm, :] = proj("c_gate") * proj("x_b")
    ec_ref[HIST_C:HIST_C + tm, :] = proj("a_c") * _sigmoid(proj("g_c"))

    pos = t * tm + lax.broadcasted_iota(jnp.int32, (tm, POOL_GROUP_DIM), 0)
    for g, win in enumerate(POOL_WINDOWS):
        lo, hi = g * POOL_GROUP_DIM, (g + 1) * POOL_GROUP_DIM
        u = ea_ref[HIST_A:HIST_A + tm, lo:hi]
        s = u
        for j in range(1, win):
            s = s + ea_ref[HIST_A - j:HIST_A - j + tm, lo:hi]
        cnt = jnp.minimum(pos + 1, win).astype(F32)
        pooled = s / cnt - u
        mixed = _dot(pooled.astype(BF16), pw_ref[g])
        mix_ref[:, lo:hi] = (mixed * ps_ref[:, lo:hi]).astype(BF16)

    conv_b = sw_ref[SCONV_K - 1:SCONV_K, :] * eb_ref[HIST_B:HIST_B + tm, :]
    for j in range(SCONV_K - 1):
        off = HIST_B - (SCONV_K - 1) + j
        conv_b = conv_b + sw_ref[j:j + 1, :] * eb_ref[off:off + tm, :]
    mix_ref[:, p_width:p_width + s_width] = (bg_ref[...] * conv_b).astype(BF16)

    tap0 = HIST_C - (CCONV_K - 1)
    span = tm + HIST_C - SUBLANES
    for c in range(c_width // LANES):
        cols = slice(c * LANES, (c + 1) * LANES)
        for r in range(1, SUBLANES):
            xs_ref[r, 0:span, :] = ec_ref[r:r + span, cols]
        wk = [jnp.broadcast_to(cw_ref[k:k + 1, cols], (SUBLANES, LANES))
              for k in range(CCONV_K)]

        def chunk(i, carry, cols=cols, wk=wk):
            base = pl.multiple_of(i * SUBLANES, SUBLANES)
            acc = [jnp.zeros((SUBLANES, LANES), F32) for _ in range(2)]
            for k in range(CCONV_K):
                q, r = divmod(tap0 + k, SUBLANES)
                if r == 0:
                    src = ec_ref[pl.ds(base + q * SUBLANES, SUBLANES), cols]
                else:
                    src = xs_ref[r, pl.ds(base + q * SUBLANES, SUBLANES), :]
                acc[k % 2] = acc[k % 2] + wk[k] * src
            cc_ref[pl.ds(base, SUBLANES), cols] = acc[0] + acc[1]
            return carry

        lax.fori_loop(0, tm // SUBLANES, chunk, 0, unroll=4)

    y_c = _layernorm_silu(cc_ref[...] + cb_ref[...], lg_ref[...], lb_ref[...])
    mix_ref[:, p_width + s_width:] = y_c.astype(BF16)

    xo_ref[0] = x + _dot(mix_ref[...], wout_ref[...])

    @pl.when(t == pl.num_programs(1) - 1)
    def _():
        nsp_ref[0] = ea_ref[HIST_A + tm - POOL_STATE:HIST_A + tm, :]
        nss_ref[0] = eb_ref[HIST_B + tm - (SCONV_K - 1):HIST_B + tm, :]
        nsc_ref[0] = ec_ref[HIST_C + tm - (CCONV_K - 1):HIST_C + tm, :]

    ea_ref[0:HIST_A, :] = ea_ref[tm:tm + HIST_A, :]
    eb_ref[0:HIST_B, :] = eb_ref[tm:tm + HIST_B, :]
    ec_ref[0:HIST_C, :] = ec_ref[tm:tm + HIST_C, :]


def _resident(block_shape, index_map):
    return pl.BlockSpec(block_shape, index_map, pipeline_mode=pl.Buffered(1))


def _prompt_mixer(x, layer, p, tm):
    nb, seq, d = x.shape
    d_in = p["w_in"].shape[2]
    p_width = p["pool_scale"].shape[2]
    s_width = p["sconv_w"].shape[2]
    c_width = p["cconv_w"].shape[2]
    sec = _in_proj_sections(p_width, s_width, c_width)
    lsel3 = lambda b, t: (layer, 0, 0)
    lsel4 = lambda b, t: (layer, 0, 0, 0)
    n_groups = len(POOL_WINDOWS)
    in_specs = [
        pl.BlockSpec((1, tm, d), lambda b, t: (b, t, 0)),
        _resident((None, 1, d), lsel3),
        _resident((None, d, d_in), lsel3),
        _resident((None, n_groups, POOL_GROUP_DIM, POOL_GROUP_DIM), lsel4),
        _resident((None, 1, p_width), lsel3),
        _resident((None, SCONV_K, s_width), lsel3),
        _resident((None, CCONV_K, c_width), lsel3),
        _resident((None, 1, c_width), lsel3),
        _resident((None, 1, c_width), lsel3),
        _resident((None, 1, c_width), lsel3),
        _resident((None, d, d), lsel3),
    ]
    out_shape = (
        jax.ShapeDtypeStruct((nb, seq, d), F32),
        jax.ShapeDtypeStruct((nb, POOL_STATE, p_width), F32),
        jax.ShapeDtypeStruct((nb, SCONV_K - 1, s_width), F32),
        jax.ShapeDtypeStruct((nb, CCONV_K - 1, c_width), F32),
    )
    out_specs = (
        pl.BlockSpec((1, tm, d), lambda b, t: (b, t, 0)),
        pl.BlockSpec((1, POOL_STATE, p_width), lambda b, t: (b, 0, 0)),
        pl.BlockSpec((1, SCONV_K - 1, s_width), lambda b, t: (b, 0, 0)),
        pl.BlockSpec((1, CCONV_K - 1, c_width), lambda b, t: (b, 0, 0)),
    )
    scratch = [
        pltpu.VMEM((HIST_A + tm, p_width), F32),
        pltpu.VMEM((HIST_B + tm, s_width), F32),
        pltpu.VMEM((HIST_C + tm, c_width), F32),
        pltpu.VMEM((SUBLANES, tm + HIST_C - SUBLANES, LANES), F32),
        pltpu.VMEM((tm, s_width), F32),
        pltpu.VMEM((tm, c_width), F32),
        pltpu.VMEM((tm, d), BF16),
    ]
    return pl.pallas_call(
        functools.partial(_prompt_mixer_kernel, tm=tm, sec=sec),
        grid=(nb, seq // tm),
        in_specs=in_specs,
        out_specs=out_specs,
        out_shape=out_shape,
        scratch_shapes=scratch,
        compiler_params=pltpu.CompilerParams(
            dimension_semantics=("parallel", "arbitrary"),
            vmem_limit_bytes=VMEM_LIMIT_BYTES),
        name=f"prompt_mixer_l{layer}",
    )(x, p["norm_mix"], p["w_in"], p["pool_w"], p["pool_scale"], p["sconv_w"],
      p["cconv_w"], p["cconv_b"], p["cconv_ln_g"], p["cconv_ln_b"], p["w_out"])


def _sample_mixer_kernel(x_ref, stp_ref, sts_ref, stc_ref, nm_ref, win_ref, pw_ref,
                         ps_ref, sw_ref, cw_ref, cb_ref, lg_ref, lb_ref, wout_ref,
                         xo_ref, nu_ref, nv_ref, nglu_ref, mix_ref, *, sec):
    p_width = nu_ref.shape[1]
    s_width = nv_ref.shape[1]
    x = x_ref[...]
    h = _rmsnorm(x, nm_ref[...]).astype(BF16)

    def proj(name):
        lo, hi = sec[name]
        return _dot(h, win_ref[:, lo:hi])

    u = proj("u_a")
    nu_ref[...] = u
    for g, win in enumerate(POOL_WINDOWS):
        lo, hi = g * POOL_GROUP_DIM, (g + 1) * POOL_GROUP_DIM
        s = u[:, lo:hi]
        for j in range(1, win):
            s = s + stp_ref[POOL_STATE - j, :, lo:hi]
        cnt = float(min(PAST_LEN + 1, win))
        pooled = s / cnt - u[:, lo:hi]
        mixed = _dot(pooled.astype(BF16), pw_ref[g])
        mix_ref[:, lo:hi] = (mixed * ps_ref[:, lo:hi]).astype(BF16)

    v = proj("c_gate") * proj("x_b")
    nv_ref[...] = v
    conv_b = sw_ref[SCONV_K - 1:SCONV_K, :] * v
    for j in range(SCONV_K - 1):
        conv_b = conv_b + sw_ref[j:j + 1, :] * sts_ref[j]
    mix_ref[:, p_width:p_width + s_width] = (proj("b_gate") * conv_b).astype(BF16)

    glu = proj("a_c") * _sigmoid(proj("g_c"))
    nglu_ref[...] = glu
    conv_c = cw_ref[CCONV_K - 1:CCONV_K, :] * glu
    for k in range(CCONV_K - 1):
        conv_c = conv_c + cw_ref[k:k + 1, :] * stc_ref[k]
    y_c = _layernorm_silu(conv_c + cb_ref[...], lg_ref[...], lb_ref[...])
    mix_ref[:, p_width + s_width:] = y_c.astype(BF16)

    xo_ref[...] = x + _dot(mix_ref[...], wout_ref[...])


def _sample_mixer(x, st_pool, st_sconv, st_cconv, layer, p):
    n, d = x.shape
    d_in = p["w_in"].shape[2]
    p_width = p["pool_scale"].shape[2]
    s_width = p["sconv_w"].shape[2]
    c_width = p["cconv_w"].shape[2]
    sec = _in_proj_sections(p_width, s_width, c_width)
    lsel3 = lambda i: (layer, 0, 0)
    lsel4 = lambda i: (layer, 0, 0, 0)
    n_groups = len(POOL_WINDOWS)
    in_specs = [
        _resident((n, d), lambda i: (0, 0)),
        _resident((None, POOL_STATE, n, p_width), lsel4),
        _resident((None, SCONV_K - 1, n, s_width), lsel4),
        _resident((None, CCONV_K - 1, n, c_width), lsel4),
        _resident((None, 1, d), lsel3),
        _resident((None, d, d_in), lsel3),
        _resident((None, n_groups, POOL_GROUP_DIM, POOL_GROUP_DIM), lsel4),
        _resident((None, 1, p_width), lsel3),
        _resident((None, SCONV_K, s_width), lsel3),
        _resident((None, CCONV_K, c_width), lsel3),
        _resident((None, 1, c_width), lsel3),
        _resident((None, 1, c_width), lsel3),
        _resident((None, 1, c_width), lsel3),
        _resident((None, d, d), lsel3),
    ]
    out_shape = (
        jax.ShapeDtypeStruct((n, d), F32),
        jax.ShapeDtypeStruct((n, p_width), F32),
        jax.ShapeDtypeStruct((n, s_width), F32),
        jax.ShapeDtypeStruct((n, c_width), F32),
    )
    out_specs = tuple(pl.BlockSpec(s.shape, lambda i: (0, 0)) for s in out_shape)
    return pl.pallas_call(
        functools.partial(_sample_mixer_kernel, sec=sec),
        grid=(1,),
        in_specs=in_specs,
        out_specs=out_specs,
        out_shape=out_shape,
        scratch_shapes=[pltpu.VMEM((n, d), BF16)],
        compiler_params=pltpu.CompilerParams(
            dimension_semantics=("arbitrary",),
            vmem_limit_bytes=VMEM_LIMIT_BYTES),
        name=f"sample_mixer_l{layer}",
    )(x, st_pool, st_sconv, st_cconv, p["norm_mix"], p["w_in"], p["pool_w"],
      p["pool_scale"], p["sconv_w"], p["cconv_w"], p["cconv_b"], p["cconv_ln_g"],
      p["cconv_ln_b"], p["w_out"])


def _ffn_kernel(x_ref, nf_ref, wg_ref, wu_ref, wd_ref, nfin_ref, o_ref, h2_ref,
                *, apply_final_norm):
    j = pl.program_id(1)

    @pl.when(j == 0)
    def _():
        x = x_ref[...]
        h2_ref[...] = _rmsnorm(x, nf_ref[...]).astype(BF16)
        o_ref[...] = x

    h2 = h2_ref[...]
    gate = _dot(h2, wg_ref[...])
    up = _dot(h2, wu_ref[...])
    f = (gate * _sigmoid(gate) * up).astype(BF16)
    o_ref[...] += _dot(f, wd_ref[...])

    if apply_final_norm:
        @pl.when(j == pl.num_programs(1) - 1)
        def _():
            o_ref[...] = _rmsnorm(o_ref[...], nfin_ref[...])


def _ffn(x, layer, p, tm, tf, apply_final_norm):
    m, d = x.shape
    f_width = p["w_gate"].shape[2]
    return pl.pallas_call(
        functools.partial(_ffn_kernel, apply_final_norm=apply_final_norm),
        grid=(m // tm, f_width // tf),
        in_specs=[
            pl.BlockSpec((tm, d), lambda i, j: (i, 0)),
            pl.BlockSpec((None, 1, d), lambda i, j: (layer, 0, 0)),
            pl.BlockSpec((None, d, tf), lambda i, j: (layer, 0, j)),
            pl.BlockSpec((None, d, tf), lambda i, j: (layer, 0, j)),
            pl.BlockSpec((None, tf, d), lambda i, j: (layer, j, 0)),
            pl.BlockSpec((1, d), lambda i, j: (0, 0)),
        ],
        out_specs=pl.BlockSpec((tm, d), lambda i, j: (i, 0)),
        out_shape=jax.ShapeDtypeStruct((m, d), F32),
        scratch_shapes=[pltpu.VMEM((tm, d), BF16)],
        compiler_params=pltpu.CompilerParams(
            dimension_semantics=("parallel", "arbitrary"),
            vmem_limit_bytes=VMEM_LIMIT_BYTES),
        name=f"ffn_l{layer}_m{m}",
    )(x, p["norm_ffn"], p["w_gate"], p["w_up"], p["w_down"], p["norm_final"])


def kernel(x_prompt, x_sample, state_pool, state_sconv, state_cconv, norm_mix, w_in,
           pool_w, pool_scale, sconv_w, cconv_w, cconv_b, cconv_ln_g, cconv_ln_b,
           w_out, norm_ffn, w_gate, w_up, w_down, norm_final):
    depth = w_in.shape[0]
    nb, seq, d = x_prompt.shape
    n_dec, dec_seq, _ = x_sample.shape
    assert dec_seq == 1
    row = lambda a: a[:, None, :]
    p = dict(
        norm_mix=row(norm_mix), w_in=w_in.astype(BF16), pool_w=pool_w.astype(BF16),
        pool_scale=row(pool_scale), sconv_w=sconv_w, cconv_w=cconv_w,
        cconv_b=row(cconv_b), cconv_ln_g=row(cconv_ln_g), cconv_ln_b=row(cconv_ln_b),
        w_out=w_out.astype(BF16), norm_ffn=row(norm_ffn), w_gate=w_gate.astype(BF16),
        w_up=w_up.astype(BF16), w_down=w_down.astype(BF16),
        norm_final=norm_final[None, :])

    mixer_tm, ffn_tm, ffn_tf = _tiles(nb * seq)
    mixer_tm = min(mixer_tm, seq)
    xp = x_prompt
    sp_p, ss_p, sc_p = [], [], []
    for l in range(depth):
        xp, nsp, nss, nsc = _prompt_mixer(xp, l, p, mixer_tm)
        sp_p.append(nsp); ss_p.append(nss); sc_p.append(nsc)
        xp = _ffn(xp.reshape(nb * seq, d), l, p, ffn_tm, ffn_tf,
                  apply_final_norm=(l == depth - 1)).reshape(nb, seq, d)

    _, s_tm, s_tf = _tiles(n_dec)
    xs = x_sample.reshape(n_dec, d)
    hist_major = lambda s: jnp.transpose(s, (0, 2, 1, 3))
    stp_t, sts_t, stc_t = hist_major(state_pool), hist_major(state_sconv), hist_major(state_cconv)
    sp_s, ss_s, sc_s = [], [], []
    for l in range(depth):
        xs, nu, nv, nglu = _sample_mixer(xs, stp_t, sts_t, stc_t, l, p)
        sp_s.append(jnp.concatenate([state_pool[l, :, 1:], nu[:, None, :]], axis=1))
        ss_s.append(jnp.concatenate([state_sconv[l, :, 1:], nv[:, None, :]], axis=1))
        sc_s.append(jnp.concatenate([state_cconv[l, :, 1:], nglu[:, None, :]], axis=1))
        xs = _ffn(xs, l, p, s_tm, s_tf, apply_final_norm=(l == depth - 1))

    return (xp, xs.reshape(n_dec, dec_seq, d),
            jnp.stack(sp_p), jnp.stack(sp_s), jnp.stack(ss_p), jnp.stack(ss_s),
            jnp.stack(sc_p), jnp.stack(sc_s))
```

```python
import functools

import jax
import jax.numpy as jnp
from jax import lax
from jax.experimental import pallas as pl
from jax.experimental.pallas import tpu as pltpu

EPS = 1e-6
PAST_LEN = 16384
POOL_WINDOWS = (2, 4, 8, 16)
POOL_GROUP_DIM = 128
POOL_STATE = 15
SCONV_K = 3
CCONV_K = 31

SUBLANES = 8
LANES = 128
BF16_ROWS = 16
HIST_A = 16
HIST_B = 8
HIST_C = 32
VMEM_LIMIT_BYTES = 58 * 1024 * 1024

BF16 = jnp.bfloat16
F32 = jnp.float32


def _tiles(n_rows):
    mixer_tm = min(256, n_rows)
    ffn_tm = min(1024, n_rows)
    ffn_tf = 512
    return mixer_tm, ffn_tm, ffn_tf


def _cast_split(n_rows, n_outer, n_inner):
    for k in range(n_inner, 0, -1):
        if n_rows % (n_outer * k) == 0 and (n_rows // (n_outer * k)) % BF16_ROWS == 0:
            return k
    raise ValueError(f"cannot split {n_rows} rows over {n_outer}x{n_inner} grid steps")


def _rmsnorm(x, g):
    ms = jnp.mean(x * x, axis=-1, keepdims=True)
    return x * lax.rsqrt(ms + EPS) * g


def _dot(a, b):
    return jnp.dot(a, b, preferred_element_type=F32)


def _sigmoid(x):
    return 1.0 / (1.0 + jnp.exp(-x))


def _layernorm_silu(c, g, b):
    mu = jnp.mean(c, axis=-1, keepdims=True)
    d = c - mu
    var = jnp.mean(d * d, axis=-1, keepdims=True)
    y = d * lax.rsqrt(var + EPS) * g + b
    return y * _sigmoid(y)


def _in_proj_sections(p_width, s_width, c_width):
    o = [0, p_width]
    for w in (s_width, s_width, s_width, c_width, c_width):
        o.append(o[-1] + w)
    names = ("u_a", "b_gate", "c_gate", "x_b", "a_c", "g_c")
    return {n: (o[i], o[i + 1]) for i, n in enumerate(names)}


def _resident(block_shape, index_map):
    return pl.BlockSpec(block_shape, index_map, pipeline_mode=pl.Buffered(1))


def _prompt_mixer_kernel(x_ref, nm_ref, win_ref, pw_ref, ps_ref, sw_ref, cw_ref, cb_ref,
                         lg_ref, lb_ref, wout_ref, wg_ref, wu_ref,
                         xo_ref, nsp_ref, nss_ref, nsc_ref, wgo_ref, wuo_ref,
                         ea_ref, eb_ref, ec_ref, xs_ref, bg_ref, cc_ref, mix_ref,
                         *, tm, sec):
    t = pl.program_id(1)
    p_width = ea_ref.shape[1]
    s_width = eb_ref.shape[1]
    c_width = ec_ref.shape[1]

    wgo_ref[...] = wg_ref[...].astype(BF16)
    wuo_ref[...] = wu_ref[...].astype(BF16)

    @pl.when(t == 0)
    def _():
        ea_ref[0:HIST_A, :] = jnp.zeros((HIST_A, p_width), F32)
        eb_ref[0:HIST_B, :] = jnp.zeros((HIST_B, s_width), F32)
        ec_ref[0:HIST_C, :] = jnp.zeros((HIST_C, c_width), F32)

    x = x_ref[0]
    h = _rmsnorm(x, nm_ref[...]).astype(BF16)

    def proj(name):
        lo, hi = sec[name]
        return _dot(h, win_ref[:, lo:hi])

    ea_ref[HIST_A:HIST_A + tm, :] = proj("u_a")
    bg_ref[...] = proj("b_gate")
    eb_ref[HIST_B:HIST_B + tm, :] = proj("c_gate") * proj("x_b")
    ec_ref[HIST_C:HIST_C + tm, :] = proj("a_c") * _sigmoid(proj("g_c"))

    pos = t * tm + lax.broadcasted_iota(jnp.int32, (tm, POOL_GROUP_DIM), 0)
    for g, win in enumerate(POOL_WINDOWS):
        lo, hi = g * POOL_GROUP_DIM, (g + 1) * POOL_GROUP_DIM
        u = ea_ref[HIST_A:HIST_A + tm, lo:hi]
        s = u
        for j in range(1, win):
            s = s + ea_ref[HIST_A - j:HIST_A - j + tm, lo:hi]
        cnt = jnp.minimum(pos + 1, win).astype(F32)
        pooled = s / cnt - u
        mixed = _dot(pooled.astype(BF16), pw_ref[g])
        mix_ref[:, lo:hi] = (mixed * ps_ref[:, lo:hi]).astype(BF16)

    conv_b = sw_ref[SCONV_K - 1:SCONV_K, :] * eb_ref[HIST_B:HIST_B + tm, :]
    for j in range(SCONV_K - 1):
        off = HIST_B - (SCONV_K - 1) + j
        conv_b = conv_b + sw_ref[j:j + 1, :] * eb_ref[off:off + tm, :]
    mix_ref[:, p_width:p_width + s_width] = (bg_ref[...] * conv_b).astype(BF16)

    tap0 = HIST_C - (CCONV_K - 1)
    span = tm + HIST_C - SUBLANES
    for c in range(c_width // LANES):
        cols = slice(c * LANES, (c + 1) * LANES)
        for r in range(1, SUBLANES):
            xs_ref[r, 0:span, :] = ec_ref[r:r + span, cols]
        wk = [jnp.broadcast_to(cw_ref[k:k + 1, cols], (SUBLANES, LANES))
              for k in range(CCONV_K)]

        def chunk(i, carry, cols=cols, wk=wk):
            base = pl.multiple_of(i * SUBLANES, SUBLANES)
            acc = [jnp.zeros((SUBLANES, LANES), F32) for _ in range(2)]
            for k in range(CCONV_K):
                q, r = divmod(tap0 + k, SUBLANES)
                if r == 0:
                    src = ec_ref[pl.ds(base + q * SUBLANES, SUBLANES), cols]
                else:
                    src = xs_ref[r, pl.ds(base + q * SUBLANES, SUBLANES), :]
                acc[k % 2] = acc[k % 2] + wk[k] * src
            cc_ref[pl.ds(base, SUBLANES), cols] = acc[0] + acc[1]
            return carry

        lax.fori_loop(0, tm // SUBLANES, chunk, 0, unroll=4)

    y_c = _layernorm_silu(cc_ref[...] + cb_ref[...], lg_ref[...], lb_ref[...])
    mix_ref[:, p_width + s_width:] = y_c.astype(BF16)

    xo_ref[0] = x + _dot(mix_ref[...], wout_ref[...])

    @pl.when(t == pl.num_programs(1) - 1)
    def _():
        nsp_ref[0] = ea_ref[HIST_A + tm - POOL_STATE:HIST_A + tm, :]
        nss_ref[0] = eb_ref[HIST_B + tm - (SCONV_K - 1):HIST_B + tm, :]
        nsc_ref[0] = ec_ref[HIST_C + tm - (CCONV_K - 1):HIST_C + tm, :]

    ea_ref[0:HIST_A, :] = ea_ref[tm:tm + HIST_A, :]
    eb_ref[0:HIST_B, :] = eb_ref[tm:tm + HIST_B, :]
    ec_ref[0:HIST_C, :] = ec_ref[tm:tm + HIST_C, :]


def _prompt_mixer(x, layer, p, win_bf, wout_bf, w_gate, w_up, tm):
    nb, seq, d = x.shape
    d_in = win_bf.shape[1]
    f_width = w_gate.shape[2]
    p_width = p["pool_scale"].shape[2]
    s_width = p["sconv_w"].shape[2]
    c_width = p["cconv_w"].shape[2]
    sec = _in_proj_sections(p_width, s_width, c_width)
    n_t = seq // tm
    cast_rows = d // (nb * n_t)
    assert cast_rows * nb * n_t == d and cast_rows % BF16_ROWS == 0
    lsel3 = lambda b, t: (layer, 0, 0)
    lsel4 = lambda b, t: (layer, 0, 0, 0)
    const2 = lambda b, t: (0, 0)
    n_groups = len(POOL_WINDOWS)
    in_specs = [
        pl.BlockSpec((1, tm, d), lambda b, t: (b, t, 0)),
        _resident((None, 1, d), lsel3),
        _resident((d, d_in), const2),
        _resident((None, n_groups, POOL_GROUP_DIM, POOL_GROUP_DIM), lsel4),
        _resident((None, 1, p_width), lsel3),
        _resident((None, SCONV_K, s_width), lsel3),
        _resident((None, CCONV_K, c_width), lsel3),
        _resident((None, 1, c_width), lsel3),
        _resident((None, 1, c_width), lsel3),
        _resident((None, 1, c_width), lsel3),
        _resident((d, d), const2),
        pl.BlockSpec((None, cast_rows, f_width), lambda b, t: (layer, b * n_t + t, 0)),
        pl.BlockSpec((None, cast_rows, f_width), lambda b, t: (layer, b * n_t + t, 0)),
    ]
    out_shape = (
        jax.ShapeDtypeStruct((nb, seq, d), F32),
        jax.ShapeDtypeStruct((nb, POOL_STATE, p_width), F32),
        jax.ShapeDtypeStruct((nb, SCONV_K - 1, s_width), F32),
        jax.ShapeDtypeStruct((nb, CCONV_K - 1, c_width), F32),
        jax.ShapeDtypeStruct((d, f_width), BF16),
        jax.ShapeDtypeStruct((d, f_width), BF16),
    )
    out_specs = (
        pl.BlockSpec((1, tm, d), lambda b, t: (b, t, 0)),
        pl.BlockSpec((1, POOL_STATE, p_width), lambda b, t: (b, 0, 0)),
        pl.BlockSpec((1, SCONV_K - 1, s_width), lambda b, t: (b, 0, 0)),
        pl.BlockSpec((1, CCONV_K - 1, c_width), lambda b, t: (b, 0, 0)),
        pl.BlockSpec((cast_rows, f_width), lambda b, t: (b * n_t + t, 0)),
        pl.BlockSpec((cast_rows, f_width), lambda b, t: (b * n_t + t, 0)),
    )
    scratch = [
        pltpu.VMEM((HIST_A + tm, p_width), F32),
        pltpu.VMEM((HIST_B + tm, s_width), F32),
        pltpu.VMEM((HIST_C + tm, c_width), F32),
        pltpu.VMEM((SUBLANES, tm + HIST_C - SUBLANES, LANES), F32),
        pltpu.VMEM((tm, s_width), F32),
        pltpu.VMEM((tm, c_width), F32),
        pltpu.VMEM((tm, d), BF16),
    ]
    return pl.pallas_call(
        functools.partial(_prompt_mixer_kernel, tm=tm, sec=sec),
        grid=(nb, n_t),
        in_specs=in_specs,
        out_specs=out_specs,
        out_shape=out_shape,
        scratch_shapes=scratch,
        compiler_params=pltpu.CompilerParams(
            dimension_semantics=("parallel", "arbitrary"),
            vmem_limit_bytes=VMEM_LIMIT_BYTES),
        name=f"prompt_mixer_l{layer}",
    )(x, p["norm_mix"], win_bf, p["pool_w"], p["pool_scale"], p["sconv_w"],
      p["cconv_w"], p["cconv_b"], p["cconv_ln_g"], p["cconv_ln_b"], wout_bf,
      w_gate, w_up)


def _sample_mixer_kernel(x_ref, stp_ref, sts_ref, stc_ref, nm_ref, win_ref, pw_ref,
                         ps_ref, sw_ref, cw_ref, cb_ref, lg_ref, lb_ref, wout_ref,
                         xo_ref, nu_ref, nv_ref, nglu_ref, mix_ref, *, sec):
    p_width = nu_ref.shape[1]
    s_width = nv_ref.shape[1]
    x = x_ref[...]
    h = _rmsnorm(x, nm_ref[...]).astype(BF16)

    def proj(name):
        lo, hi = sec[name]
        return _dot(h, win_ref[:, lo:hi])

    u = proj("u_a")
    nu_ref[...] = u
    for g, win in enumerate(POOL_WINDOWS):
        lo, hi = g * POOL_GROUP_DIM, (g + 1) * POOL_GROUP_DIM
        s = u[:, lo:hi]
        for j in range(1, win):
            s = s + stp_ref[:, POOL_STATE - j, lo:hi]
        cnt = float(min(PAST_LEN + 1, win))
        pooled = s / cnt - u[:, lo:hi]
        mixed = _dot(pooled.astype(BF16), pw_ref[g])
        mix_ref[:, lo:hi] = (mixed * ps_ref[:, lo:hi]).astype(BF16)

    v = proj("c_gate") * proj("x_b")
    nv_ref[...] = v
    conv_b = sw_ref[SCONV_K - 1:SCONV_K, :] * v
    for j in range(SCONV_K - 1):
        conv_b = conv_b + sw_ref[j:j + 1, :] * sts_ref[:, j, :]
    mix_ref[:, p_width:p_width + s_width] = (proj("b_gate") * conv_b).astype(BF16)

    glu = proj("a_c") * _sigmoid(proj("g_c"))
    nglu_ref[...] = glu
    conv_c = cw_ref[CCONV_K - 1:CCONV_K, :] * glu
    for k in range(CCONV_K - 1):
        conv_c = conv_c + cw_ref[k:k + 1, :] * stc_ref[:, k, :]
    y_c = _layernorm_silu(conv_c + cb_ref[...], lg_ref[...], lb_ref[...])
    mix_ref[:, p_width + s_width:] = y_c.astype(BF16)

    xo_ref[...] = x + _dot(mix_ref[...], wout_ref[...])


def _sample_mixer(x, states, layer, p, win_bf, wout_bf):
    n, d = x.shape
    d_in = win_bf.shape[1]
    p_width = p["pool_scale"].shape[2]
    s_width = p["sconv_w"].shape[2]
    c_width = p["cconv_w"].shape[2]
    sec = _in_proj_sections(p_width, s_width, c_width)
    lsel3 = lambda i: (layer, 0, 0)
    lsel4 = lambda i: (layer, 0, 0, 0)
    const2 = lambda i: (0, 0)
    n_groups = len(POOL_WINDOWS)
    in_specs = [
        _resident((n, d), const2),
        _resident((None, n, POOL_STATE, p_width), lsel4),
        _resident((None, n, SCONV_K - 1, s_width), lsel4),
        _resident((None, n, CCONV_K - 1, c_width), lsel4),
        _resident((None, 1, d), lsel3),
        _resident((d, d_in), const2),
        _resident((None, n_groups, POOL_GROUP_DIM, POOL_GROUP_DIM), lsel4),
        _resident((None, 1, p_width), lsel3),
        _resident((None, SCONV_K, s_width), lsel3),
        _resident((None, CCONV_K, c_width), lsel3),
        _resident((None, 1, c_width), lsel3),
        _resident((None, 1, c_width), lsel3),
        _resident((None, 1, c_width), lsel3),
        _resident((d, d), const2),
    ]
    out_shape = (
        jax.ShapeDtypeStruct((n, d), F32),
        jax.ShapeDtypeStruct((n, p_width), F32),
        jax.ShapeDtypeStruct((n, s_width), F32),
        jax.ShapeDtypeStruct((n, c_width), F32),
    )
    out_specs = tuple(pl.BlockSpec(s.shape, const2) for s in out_shape)
    return pl.pallas_call(
        functools.partial(_sample_mixer_kernel, sec=sec),
        grid=(1,),
        in_specs=in_specs,
        out_specs=out_specs,
        out_shape=out_shape,
        scratch_shapes=[pltpu.VMEM((n, d), BF16)],
        compiler_params=pltpu.CompilerParams(
            dimension_semantics=("arbitrary",),
            vmem_limit_bytes=VMEM_LIMIT_BYTES),
        name=f"sample_mixer_l{layer}",
    )(x, *states, p["norm_mix"], win_bf, p["pool_w"], p["pool_scale"], p["sconv_w"],
      p["cconv_w"], p["cconv_b"], p["cconv_ln_g"], p["cconv_ln_b"], wout_bf)


def _state_update_kernel(*refs):
    n_states = len(refs) // 3
    for st_ref, new_ref, out_ref in zip(refs[:n_states], refs[n_states:2 * n_states],
                                        refs[2 * n_states:]):
        keep = st_ref.shape[1] - 1
        out_ref[:, 0:keep, :] = st_ref[:, 1:keep + 1, :]
        out_ref[:, keep, :] = new_ref[...]


def _state_update(states, new_rows, chunk):
    depth, n = states[0].shape[:2]
    st_specs = [pl.BlockSpec((None, chunk) + s.shape[2:], lambda l, c: (l, c, 0, 0))
                for s in states]
    new_specs = [pl.BlockSpec((None, chunk, r.shape[2]), lambda l, c: (l, c, 0))
                 for r in new_rows]
    return pl.pallas_call(
        _state_update_kernel,
        grid=(depth, n // chunk),
        in_specs=st_specs + new_specs,
        out_specs=st_specs,
        out_shape=[jax.ShapeDtypeStruct(s.shape, s.dtype) for s in states],
        compiler_params=pltpu.CompilerParams(
            dimension_semantics=("parallel", "parallel"),
            vmem_limit_bytes=VMEM_LIMIT_BYTES),
        name="sample_state_update",
    )(*states, *new_rows)


def _ffn_kernel(*refs, apply_final_norm, n_cast):
    x_ref, nf_ref, wg_ref, wu_ref, wd_ref, nfin_ref = refs[:6]
    cast_in = refs[6:6 + n_cast]
    o_ref = refs[6 + n_cast]
    cast_out = refs[7 + n_cast:7 + 2 * n_cast]
    h2_ref = refs[7 + 2 * n_cast]
    j = pl.program_id(1)

    for src, dst in zip(cast_in, cast_out):
        dst[...] = src[...].astype(BF16)

    @pl.when(j == 0)
    def _():
        x = x_ref[...]
        h2_ref[...] = _rmsnorm(x, nf_ref[...]).astype(BF16)
        o_ref[...] = x

    h2 = h2_ref[...]
    gate = _dot(h2, wg_ref[...])
    up = _dot(h2, wu_ref[...])
    f = (gate * _sigmoid(gate) * up).astype(BF16)
    o_ref[...] += _dot(f, wd_ref[...])

    if apply_final_norm:
        @pl.when(j == pl.num_programs(1) - 1)
        def _():
            o_ref[...] = _rmsnorm(o_ref[...], nfin_ref[...])


def _ffn(x, layer, p, wg_bf, wu_bf, wd_bf, tm, tf, apply_final_norm, cast_next=()):
    m, d = x.shape
    f_width = wg_bf.shape[1]
    n_i, n_j = m // tm, f_width // tf
    in_specs = [
        pl.BlockSpec((tm, d), lambda i, j: (i, 0)),
        pl.BlockSpec((None, 1, d), lambda i, j: (layer, 0, 0)),
        pl.BlockSpec((d, tf), lambda i, j: (0, j)),
        pl.BlockSpec((d, tf), lambda i, j: (0, j)),
        pl.BlockSpec((tf, d), lambda i, j: (j, 0)),
        pl.BlockSpec((1, d), lambda i, j: (0, 0)),
    ]
    out_specs = [pl.BlockSpec((tm, d), lambda i, j: (i, 0))]
    out_shape = [jax.ShapeDtypeStruct((m, d), F32)]
    for w in cast_next:
        _, rows, cols = w.shape
        k = _cast_split(rows, n_i, n_j)
        blk = rows // (n_i * k)
        step = lambda i, j, k=k: i * k + jnp.minimum(j, k - 1)
        in_specs.append(pl.BlockSpec((None, blk, cols),
                                     lambda i, j, step=step: (layer + 1, step(i, j), 0)))
        out_specs.append(pl.BlockSpec((blk, cols), lambda i, j, step=step: (step(i, j), 0)))
        out_shape.append(jax.ShapeDtypeStruct((rows, cols), BF16))
    return pl.pallas_call(
        functools.partial(_ffn_kernel, apply_final_norm=apply_final_norm,
                          n_cast=len(cast_next)),
        grid=(n_i, n_j),
        in_specs=in_specs,
        out_specs=out_specs,
        out_shape=out_shape,
        scratch_shapes=[pltpu.VMEM((tm, d), BF16)],
        compiler_params=pltpu.CompilerParams(
            dimension_semantics=("parallel", "arbitrary"),
            vmem_limit_bytes=VMEM_LIMIT_BYTES),
        name=f"ffn_l{layer}_m{m}",
    )(x, p["norm_ffn"], wg_bf, wu_bf, wd_bf, p["norm_final"], *cast_next)


def kernel(x_prompt, x_sample, state_pool, state_sconv, state_cconv, norm_mix, w_in,
           pool_w, pool_scale, sconv_w, cconv_w, cconv_b, cconv_ln_g, cconv_ln_b,
           w_out, norm_ffn, w_gate, w_up, w_down, norm_final):
    depth = w_in.shape[0]
    nb, seq, d = x_prompt.shape
    n_dec, dec_seq, _ = x_sample.shape
    assert dec_seq == 1
    row = lambda a: a[:, None, :]
    p = dict(
        norm_mix=row(norm_mix), pool_w=pool_w.astype(BF16), pool_scale=row(pool_scale),
        sconv_w=sconv_w, cconv_w=cconv_w, cconv_b=row(cconv_b),
        cconv_ln_g=row(cconv_ln_g), cconv_ln_b=row(cconv_ln_b), norm_ffn=row(norm_ffn),
        norm_final=norm_final[None, :])

    mixer_tm, ffn_tm, ffn_tf = _tiles(nb * seq)
    mixer_tm = min(mixer_tm, seq)
    _, s_tm, s_tf = _tiles(n_dec)

    win_bf, wout_bf, wd_bf = (w[0].astype(BF16) for w in (w_in, w_out, w_down))

    xp = x_prompt
    xs = x_sample.reshape(n_dec, d)
    states = (state_pool, state_sconv, state_cconv)
    prompt_states = ([], [], [])
    sample_rows = ([], [], [])
    for l in range(depth):
        last = l == depth - 1
        xp, nsp, nss, nsc, wg_bf, wu_bf = _prompt_mixer(
            xp, l, p, win_bf, wout_bf, w_gate, w_up, mixer_tm)
        for acc, s in zip(prompt_states, (nsp, nss, nsc)):
            acc.append(s)
        outs = _ffn(xp.reshape(nb * seq, d), l, p, wg_bf, wu_bf, wd_bf, ffn_tm, ffn_tf,
                    apply_final_norm=last, cast_next=() if last else (w_in, w_out, w_down))
        xp = outs[0].reshape(nb, seq, d)
        xs, nu, nv, nglu = _sample_mixer(xs, states, l, p, win_bf, wout_bf)
        for acc, r in zip(sample_rows, (nu, nv, nglu)):
            acc.append(r)
        xs = _ffn(xs, l, p, wg_bf, wu_bf, wd_bf, s_tm, s_tf, apply_final_norm=last)[0]
        if not last:
            win_bf, wout_bf, wd_bf = outs[1:]

    sp_p, ss_p, sc_p = (jnp.stack(s) for s in prompt_states)
    sp_s, ss_s, sc_s = _state_update(states, [jnp.stack(r) for r in sample_rows],
                                     chunk=min(16, n_dec))
    return (xp, xs.reshape(n_dec, dec_seq, d), sp_p, sp_s, ss_p, ss_s, sc_p, sc_s)
```

```python
import functools

import jax
import jax.numpy as jnp
from jax import lax
from jax.experimental import pallas as pl
from jax.experimental.pallas import tpu as pltpu

EPS = 1e-6
PAST_LEN = 16384
POOL_WINDOWS = (2, 4, 8, 16)
POOL_GROUP_DIM = 128
POOL_STATE = 15
SCONV_K = 3
CCONV_K = 31

SUBLANES = 8
LANES = 128
BF16_ROWS = 16
HIST_A = 16
HIST_B = 8
HIST_C = 32
VMEM_LIMIT_BYTES = 58 * 1024 * 1024
N_STATES = 3

BF16 = jnp.bfloat16
F32 = jnp.float32
ANY_SPEC = pl.BlockSpec(memory_space=pl.ANY)


def _tiles(n_prompt_rows, n_sample_rows, seq):
    mixer_tm = min(256, seq)
    ffn_row_tiles = max(1, n_prompt_rows // 1024)
    assert n_prompt_rows % ffn_row_tiles == 0 and n_sample_rows % ffn_row_tiles == 0
    assert ((n_prompt_rows + n_sample_rows) // ffn_row_tiles) % BF16_ROWS == 0
    ffn_tf = 512
    return mixer_tm, ffn_row_tiles, ffn_tf


def _cast_split(n_rows, n_outer, n_inner):
    for k in range(n_inner, 0, -1):
        if n_rows % (n_outer * k) == 0 and (n_rows // (n_outer * k)) % BF16_ROWS == 0:
            return k
    raise ValueError(f"cannot split {n_rows} rows over {n_outer}x{n_inner} grid steps")


def _rmsnorm(x, g):
    ms = jnp.mean(x * x, axis=-1, keepdims=True)
    return x * lax.rsqrt(ms + EPS) * g


def _dot(a, b):
    return jnp.dot(a, b, preferred_element_type=F32)


def _sigmoid(x):
    return 1.0 / (1.0 + jnp.exp(-x))


def _layernorm_silu(c, g, b):
    mu = jnp.mean(c, axis=-1, keepdims=True)
    d = c - mu
    var = jnp.mean(d * d, axis=-1, keepdims=True)
    y = d * lax.rsqrt(var + EPS) * g + b
    return y * _sigmoid(y)


def _in_proj_sections(p_width, s_width, c_width):
    o = [0, p_width]
    for w in (s_width, s_width, s_width, c_width, c_width):
        o.append(o[-1] + w)
    names = ("u_a", "b_gate", "c_gate", "x_b", "a_c", "g_c")
    return {n: (o[i], o[i + 1]) for i, n in enumerate(names)}


def _resident(block_shape, index_map):
    return pl.BlockSpec(block_shape, index_map, pipeline_mode=pl.Buffered(1))


def _shift_copy(src_ref, dst_ref, sem):
    keep = src_ref.shape[1] - 1
    return pltpu.make_async_copy(src_ref.at[:, pl.ds(1, keep)],
                                 dst_ref.at[:, pl.ds(0, keep)], sem)


def _prompt_mixer_kernel(*refs, tm, sec, shift_states):
    n_in = 14 + (N_STATES if shift_states else 0)
    n_out = 7 + (N_STATES if shift_states else 0)
    (x_ref, nm_ref, win_ref, pw_ref, ps_ref, sw_ref, cw_ref, cb_ref, lg_ref, lb_ref,
     wout_ref, wg_ref, wu_ref, wd_ref) = refs[:14]
    st_refs = refs[14:n_in]
    xo_ref, nsp_ref, nss_ref, nsc_ref, wgo_ref, wuo_ref, wdo_ref = refs[n_in:n_in + 7]
    buf_refs = refs[n_in + 7:n_in + n_out]
    ea_ref, eb_ref, ec_ref, xs_ref, bg_ref, cc_ref, mix_ref = refs[n_in + n_out:n_in + n_out + 7]

    b = pl.program_id(0)
    t = pl.program_id(1)
    p_width = ea_ref.shape[1]
    s_width = eb_ref.shape[1]
    c_width = ec_ref.shape[1]

    if shift_states:
        shift_sem = refs[n_in + n_out + 7]
        shifts = [_shift_copy(st_refs[i], buf_refs[i], shift_sem.at[i])
                  for i in range(N_STATES)]

        @pl.when((b == 0) & (t == 0))
        def _():
            for cp in shifts:
                cp.start()

    wgo_ref[...] = wg_ref[...].astype(BF16)
    wuo_ref[...] = wu_ref[...].astype(BF16)
    wdo_ref[...] = wd_ref[...].astype(BF16)

    @pl.when(t == 0)
    def _():
        ea_ref[0:HIST_A, :] = jnp.zeros((HIST_A, p_width), F32)
        eb_ref[0:HIST_B, :] = jnp.zeros((HIST_B, s_width), F32)
        ec_ref[0:HIST_C, :] = jnp.zeros((HIST_C, c_width), F32)

    x = x_ref[...]
    h = _rmsnorm(x, nm_ref[...]).astype(BF16)

    def proj(name):
        lo, hi = sec[name]
        return _dot(h, win_ref[:, lo:hi])

    ea_ref[HIST_A:HIST_A + tm, :] = proj("u_a")
    bg_ref[...] = proj("b_gate")
    eb_ref[HIST_B:HIST_B + tm, :] = proj("c_gate") * proj("x_b")
    ec_ref[HIST_C:HIST_C + tm, :] = proj("a_c") * _sigmoid(proj("g_c"))

    pos = t * tm + lax.broadcasted_iota(jnp.int32, (tm, POOL_GROUP_DIM), 0)
    for g, win in enumerate(POOL_WINDOWS):
        lo, hi = g * POOL_GROUP_DIM, (g + 1) * POOL_GROUP_DIM
        u = ea_ref[HIST_A:HIST_A + tm, lo:hi]
        s = u
        for j in range(1, win):
            s = s + ea_ref[HIST_A - j:HIST_A - j + tm, lo:hi]
        cnt = jnp.minimum(pos + 1, win).astype(F32)
        pooled = s / cnt - u
        mixed = _dot(pooled.astype(BF16), pw_ref[g])
        mix_ref[:, lo:hi] = (mixed * ps_ref[:, lo:hi]).astype(BF16)

    conv_b = sw_ref[SCONV_K - 1:SCONV_K, :] * eb_ref[HIST_B:HIST_B + tm, :]
    for j in range(SCONV_K - 1):
        off = HIST_B - (SCONV_K - 1) + j
        conv_b = conv_b + sw_ref[j:j + 1, :] * eb_ref[off:off + tm, :]
    mix_ref[:, p_width:p_width + s_width] = (bg_ref[...] * conv_b).astype(BF16)

    tap0 = HIST_C - (CCONV_K - 1)
    span = tm + HIST_C - SUBLANES
    for c in range(c_width // LANES):
        cols = slice(c * LANES, (c + 1) * LANES)
        for r in range(1, SUBLANES):
            xs_ref[r, 0:span, :] = ec_ref[r:r + span, cols]
        wk = [jnp.broadcast_to(cw_ref[k:k + 1, cols], (SUBLANES, LANES))
              for k in range(CCONV_K)]

        def chunk(i, carry, cols=cols, wk=wk):
            base = pl.multiple_of(i * SUBLANES, SUBLANES)
            acc = [jnp.zeros((SUBLANES, LANES), F32) for _ in range(2)]
            for k in range(CCONV_K):
                q, r = divmod(tap0 + k, SUBLANES)
                if r == 0:
                    src = ec_ref[pl.ds(base + q * SUBLANES, SUBLANES), cols]
                else:
                    src = xs_ref[r, pl.ds(base + q * SUBLANES, SUBLANES), :]
                acc[k % 2] = acc[k % 2] + wk[k] * src
            cc_ref[pl.ds(base, SUBLANES), cols] = acc[0] + acc[1]
            return carry

        lax.fori_loop(0, tm // SUBLANES, chunk, 0, unroll=4)

    y_c = _layernorm_silu(cc_ref[...] + cb_ref[...], lg_ref[...], lb_ref[...])
    mix_ref[:, p_width + s_width:] = y_c.astype(BF16)

    xo_ref[...] = x + _dot(mix_ref[...], wout_ref[...])

    last_t = t == pl.num_programs(1) - 1

    @pl.when(last_t)
    def _():
        nsp_ref[0] = ea_ref[HIST_A + tm - POOL_STATE:HIST_A + tm, :]
        nss_ref[0] = eb_ref[HIST_B + tm - (SCONV_K - 1):HIST_B + tm, :]
        nsc_ref[0] = ec_ref[HIST_C + tm - (CCONV_K - 1):HIST_C + tm, :]

    ea_ref[0:HIST_A, :] = ea_ref[tm:tm + HIST_A, :]
    eb_ref[0:HIST_B, :] = eb_ref[tm:tm + HIST_B, :]
    ec_ref[0:HIST_C, :] = ec_ref[tm:tm + HIST_C, :]

    if shift_states:
        @pl.when((b == pl.num_programs(0) - 1) & last_t)
        def _():
            for cp in shifts:
                cp.wait()


def _prompt_mixer(x, n_rows_out, nb, seq, layer, p, win_bf, wout_bf, ffn_w, tm,
                  sample_states=()):
    d = x.shape[1]
    d_in = win_bf.shape[1]
    p_width = p["pool_scale"].shape[2]
    s_width = p["sconv_w"].shape[2]
    c_width = p["cconv_w"].shape[2]
    sec = _in_proj_sections(p_width, s_width, c_width)
    n_t = seq // tm
    n_steps = nb * n_t
    shift_states = bool(sample_states)
    lsel3 = lambda b, t: (layer, 0, 0)
    lsel4 = lambda b, t: (layer, 0, 0, 0)
    const2 = lambda b, t: (0, 0)
    step = lambda b, t: b * n_t + t
    n_groups = len(POOL_WINDOWS)
    in_specs = [
        pl.BlockSpec((tm, d), lambda b, t: (step(b, t), 0)),
        _resident((None, 1, d), lsel3),
        _resident((d, d_in), const2),
        _resident((None, n_groups, POOL_GROUP_DIM, POOL_GROUP_DIM), lsel4),
        _resident((None, 1, p_width), lsel3),
        _resident((None, SCONV_K, s_width), lsel3),
        _resident((None, CCONV_K, c_width), lsel3),
        _resident((None, 1, c_width), lsel3),
        _resident((None, 1, c_width), lsel3),
        _resident((None, 1, c_width), lsel3),
        _resident((d, d), const2),
    ]
    out_shape = [
        jax.ShapeDtypeStruct((n_rows_out, d), F32),
        jax.ShapeDtypeStruct((nb, POOL_STATE, p_width), F32),
        jax.ShapeDtypeStruct((nb, SCONV_K - 1, s_width), F32),
        jax.ShapeDtypeStruct((nb, CCONV_K - 1, c_width), F32),
    ]
    out_specs = [
        pl.BlockSpec((tm, d), lambda b, t: (step(b, t), 0)),
        pl.BlockSpec((1, POOL_STATE, p_width), lambda b, t: (b, 0, 0)),
        pl.BlockSpec((1, SCONV_K - 1, s_width), lambda b, t: (b, 0, 0)),
        pl.BlockSpec((1, CCONV_K - 1, c_width), lambda b, t: (b, 0, 0)),
    ]
    for w in ffn_w:
        _, rows, cols = w.shape
        blk = rows // n_steps
        assert blk * n_steps == rows and blk % BF16_ROWS == 0
        in_specs.append(pl.BlockSpec((None, blk, cols), lambda b, t: (layer, step(b, t), 0)))
        out_specs.append(pl.BlockSpec((blk, cols), lambda b, t: (step(b, t), 0)))
        out_shape.append(jax.ShapeDtypeStruct((rows, cols), BF16))
    scratch = [
        pltpu.VMEM((HIST_A + tm, p_width), F32),
        pltpu.VMEM((HIST_B + tm, s_width), F32),
        pltpu.VMEM((HIST_C + tm, c_width), F32),
        pltpu.VMEM((SUBLANES, tm + HIST_C - SUBLANES, LANES), F32),
        pltpu.VMEM((tm, s_width), F32),
        pltpu.VMEM((tm, c_width), F32),
        pltpu.VMEM((tm, d), BF16),
    ]
    if shift_states:
        in_specs += [ANY_SPEC] * N_STATES
        out_specs += [ANY_SPEC] * N_STATES
        out_shape += [jax.ShapeDtypeStruct(s.shape, s.dtype) for s in sample_states]
        scratch.append(pltpu.SemaphoreType.DMA((N_STATES,)))
    return pl.pallas_call(
        functools.partial(_prompt_mixer_kernel, tm=tm, sec=sec, shift_states=shift_states),
        grid=(nb, n_t),
        in_specs=in_specs,
        out_specs=out_specs,
        out_shape=out_shape,
        scratch_shapes=scratch,
        compiler_params=pltpu.CompilerParams(
            dimension_semantics=("arbitrary", "arbitrary"),
            vmem_limit_bytes=VMEM_LIMIT_BYTES),
        name=f"prompt_mixer_l{layer}",
    )(x, p["norm_mix"], win_bf, p["pool_w"], p["pool_scale"], p["sconv_w"],
      p["cconv_w"], p["cconv_b"], p["cconv_ln_g"], p["cconv_ln_b"], wout_bf,
      *ffn_w, *sample_states)


def _sample_mixer_kernel(x_ref, stp_ref, sts_ref, stc_ref, xbuf_in, bufp_in, bufs_in,
                         bufc_in, nm_ref, win_ref, pw_ref, ps_ref, sw_ref, cw_ref,
                         cb_ref, lg_ref, lb_ref, wout_ref,
                         xo_ref, nu_ref, nv_ref, nglu_ref, mix_ref, *, sec):
    del xbuf_in, bufp_in, bufs_in, bufc_in
    p_width = nu_ref.shape[1]
    s_width = nv_ref.shape[1]
    x = x_ref[...]
    h = _rmsnorm(x, nm_ref[...]).astype(BF16)

    def proj(name):
        lo, hi = sec[name]
        return _dot(h, win_ref[:, lo:hi])

    u = proj("u_a")
    nu_ref[...] = u
    for g, win in enumerate(POOL_WINDOWS):
        lo, hi = g * POOL_GROUP_DIM, (g + 1) * POOL_GROUP_DIM
        s = u[:, lo:hi]
        for j in range(1, win):
            s = s + stp_ref[POOL_STATE - j, :, lo:hi]
        cnt = float(min(PAST_LEN + 1, win))
        pooled = s / cnt - u[:, lo:hi]
        mixed = _dot(pooled.astype(BF16), pw_ref[g])
        mix_ref[:, lo:hi] = (mixed * ps_ref[:, lo:hi]).astype(BF16)

    v = proj("c_gate") * proj("x_b")
    nv_ref[...] = v
    conv_b = sw_ref[SCONV_K - 1:SCONV_K, :] * v
    for j in range(SCONV_K - 1):
        conv_b = conv_b + sw_ref[j:j + 1, :] * sts_ref[j]
    mix_ref[:, p_width:p_width + s_width] = (proj("b_gate") * conv_b).astype(BF16)

    glu = proj("a_c") * _sigmoid(proj("g_c"))
    nglu_ref[...] = glu
    conv_c = cw_ref[CCONV_K - 1:CCONV_K, :] * glu
    for k in range(CCONV_K - 1):
        conv_c = conv_c + cw_ref[k:k + 1, :] * stc_ref[k]
    y_c = _layernorm_silu(conv_c + cb_ref[...], lg_ref[...], lb_ref[...])
    mix_ref[:, p_width + s_width:] = y_c.astype(BF16)

    xo_ref[...] = x + _dot(mix_ref[...], wout_ref[...])


def _sample_mixer(x, x_block, xbuf, states, bufs, layer, p, win_bf, wout_bf):
    d = x.shape[1]
    _, k_pool, n, p_width = states[0].shape
    s_width = states[1].shape[3]
    c_width = states[2].shape[3]
    d_in = win_bf.shape[1]
    sec = _in_proj_sections(p_width, s_width, c_width)
    lsel3 = lambda i: (layer, 0, 0)
    lsel4 = lambda i: (layer, 0, 0, 0)
    const2 = lambda i: (0, 0)
    n_groups = len(POOL_WINDOWS)
    assert xbuf.shape[0] % n == 0
    out_block = xbuf.shape[0] // n - 1
    in_specs = [_resident((n, d), lambda i: (x_block, 0))] + [
        _resident((None,) + s.shape[1:], lsel4) for s in states
    ] + [ANY_SPEC] * (1 + N_STATES) + [
        _resident((None, 1, d), lsel3),
        _resident((d, d_in), const2),
        _resident((None, n_groups, POOL_GROUP_DIM, POOL_GROUP_DIM), lsel4),
        _resident((None, 1, p_width), lsel3),
        _resident((None, SCONV_K, s_width), lsel3),
        _resident((None, CCONV_K, c_width), lsel3),
        _resident((None, 1, c_width), lsel3),
        _resident((None, 1, c_width), lsel3),
        _resident((None, 1, c_width), lsel3),
        _resident((d, d), const2),
    ]
    out_shape = [jax.ShapeDtypeStruct(a.shape, a.dtype) for a in (xbuf, *bufs)]
    out_specs = [pl.BlockSpec((n, d), lambda i: (out_block, 0))] + [
        pl.BlockSpec((None, None, n, b.shape[3]), lambda i, b=b: (layer, b.shape[1] - 1, 0, 0))
        for b in bufs
    ]
    return pl.pallas_call(
        functools.partial(_sample_mixer_kernel, sec=sec),
        grid=(1,),
        in_specs=in_specs,
        out_specs=out_specs,
        out_shape=out_shape,
        scratch_shapes=[pltpu.VMEM((n, d), BF16)],
        input_output_aliases={1 + N_STATES + i: i for i in range(1 + N_STATES)},
        compiler_params=pltpu.CompilerParams(
            dimension_semantics=("arbitrary",),
            vmem_limit_bytes=VMEM_LIMIT_BYTES),
        name=f"sample_mixer_l{layer}",
    )(x, *states, xbuf, *bufs, p["norm_mix"], win_bf, p["pool_w"], p["pool_scale"],
      p["sconv_w"], p["cconv_w"], p["cconv_b"], p["cconv_ln_g"], p["cconv_ln_b"],
      wout_bf)


def _ffn_kernel(*refs, apply_final_norm, n_cast):
    x_ref, nf_ref, wg_ref, wu_ref, wd_ref, nfin_ref = refs[:6]
    cast_in = refs[6:6 + n_cast]
    o_ref = refs[6 + n_cast]
    cast_out = refs[7 + n_cast:7 + 2 * n_cast]
    h2_ref = refs[7 + 2 * n_cast]
    j = pl.program_id(1)

    for src, dst in zip(cast_in, cast_out):
        dst[...] = src[...].astype(BF16)

    @pl.when(j == 0)
    def _():
        x = x_ref[...]
        h2_ref[...] = _rmsnorm(x, nf_ref[...]).astype(BF16)
        o_ref[...] = x

    h2 = h2_ref[...]
    gate = _dot(h2, wg_ref[...])
    up = _dot(h2, wu_ref[...])
    f = (gate * _sigmoid(gate) * up).astype(BF16)
    o_ref[...] += _dot(f, wd_ref[...])

    if apply_final_norm:
        @pl.when(j == pl.num_programs(1) - 1)
        def _():
            o_ref[...] = _rmsnorm(o_ref[...], nfin_ref[...])


def _ffn(x, first_block, n_i, tm, layer, p, wg_bf, wu_bf, wd_bf, tf, apply_final_norm,
         cast_next=()):
    d = x.shape[1]
    f_width = wg_bf.shape[1]
    n_j = f_width // tf
    in_specs = [
        pl.BlockSpec((tm, d), lambda i, j: (first_block + i, 0)),
        pl.BlockSpec((None, 1, d), lambda i, j: (layer, 0, 0)),
        pl.BlockSpec((d, tf), lambda i, j: (0, j)),
        pl.BlockSpec((d, tf), lambda i, j: (0, j)),
        pl.BlockSpec((tf, d), lambda i, j: (j, 0)),
        pl.BlockSpec((1, d), lambda i, j: (0, 0)),
    ]
    out_specs = [pl.BlockSpec((tm, d), lambda i, j: (i, 0))]
    out_shape = [jax.ShapeDtypeStruct((n_i * tm, d), F32)]
    for w in cast_next:
        _, rows, cols = w.shape
        k = _cast_split(rows, n_i, n_j)
        blk = rows // (n_i * k)
        step = lambda i, j, k=k: i * k + jnp.minimum(j, k - 1)
        in_specs.append(pl.BlockSpec((None, blk, cols),
                                     lambda i, j, step=step: (layer + 1, step(i, j), 0)))
        out_specs.append(pl.BlockSpec((blk, cols), lambda i, j, step=step: (step(i, j), 0)))
        out_shape.append(jax.ShapeDtypeStruct((rows, cols), BF16))
    return pl.pallas_call(
        functools.partial(_ffn_kernel, apply_final_norm=apply_final_norm,
                          n_cast=len(cast_next)),
        grid=(n_i, n_j),
        in_specs=in_specs,
        out_specs=out_specs,
        out_shape=out_shape,
        scratch_shapes=[pltpu.VMEM((tm, d), BF16)],
        compiler_params=pltpu.CompilerParams(
            dimension_semantics=("parallel", "arbitrary"),
            vmem_limit_bytes=VMEM_LIMIT_BYTES),
        name=f"ffn_l{layer}_m{n_i * tm}",
    )(x, p["norm_ffn"], wg_bf, wu_bf, wd_bf, p["norm_final"], *cast_next)


def kernel(x_prompt, x_sample, state_pool, state_sconv, state_cconv, norm_mix, w_in,
           pool_w, pool_scale, sconv_w, cconv_w, cconv_b, cconv_ln_g, cconv_ln_b,
           w_out, norm_ffn, w_gate, w_up, w_down, norm_final):
    depth = w_in.shape[0]
    nb, seq, d = x_prompt.shape
    n_dec, dec_seq, _ = x_sample.shape
    assert dec_seq == 1
    n_prompt = nb * seq
    n_rows = n_prompt + n_dec
    assert n_prompt % n_dec == 0
    row = lambda a: a[:, None, :]
    p = dict(
        norm_mix=row(norm_mix), pool_w=pool_w.astype(BF16), pool_scale=row(pool_scale),
        sconv_w=sconv_w, cconv_w=cconv_w, cconv_b=row(cconv_b),
        cconv_ln_g=row(cconv_ln_g), cconv_ln_b=row(cconv_ln_b), norm_ffn=row(norm_ffn),
        norm_final=norm_final[None, :])

    mixer_tm, ffn_tiles, ffn_tf = _tiles(n_prompt, n_dec, seq)

    win_bf, wout_bf = w_in[0].astype(BF16), w_out[0].astype(BF16)

    hist_major = lambda s: jnp.transpose(s, (0, 2, 1, 3))
    states = tuple(hist_major(s) for s in (state_pool, state_sconv, state_cconv))

    x_p = x_prompt.reshape(n_prompt, d)
    x_s, x_s_block = x_sample.reshape(n_dec, d), 0
    bufs = None
    prompt_states = ([], [], [])
    for l in range(depth):
        last = l == depth - 1
        outs = _prompt_mixer(x_p, n_rows, nb, seq, l, p, win_bf, wout_bf,
                             (w_gate, w_up, w_down), mixer_tm,
                             sample_states=states if l == 0 else ())
        xbuf, nsp, nss, nsc, wg_bf, wu_bf, wd_bf = outs[:7]
        if l == 0:
            bufs = outs[7:]
        for acc, s in zip(prompt_states, (nsp, nss, nsc)):
            acc.append(s)
        xbuf, *bufs = _sample_mixer(x_s, x_s_block, xbuf, states, bufs, l, p, win_bf, wout_bf)
        if not last:
            outs = _ffn(xbuf, 0, ffn_tiles, n_rows // ffn_tiles, l, p, wg_bf, wu_bf, wd_bf,
                        ffn_tf, apply_final_norm=False, cast_next=(w_in, w_out))
            x_p = x_s = outs[0]
            x_s_block = n_prompt // n_dec
            win_bf, wout_bf = outs[1:]
        else:
            y_p = _ffn(xbuf, 0, ffn_tiles, n_prompt // ffn_tiles, l, p, wg_bf, wu_bf, wd_bf,
                       ffn_tf, apply_final_norm=True)[0]
            y_s = _ffn(xbuf, n_prompt // n_dec, 1, n_dec, l, p, wg_bf, wu_bf, wd_bf,
                       ffn_tf, apply_final_norm=True)[0]

    sp_p, ss_p, sc_p = (jnp.stack(s) for s in prompt_states)
    sp_s, ss_s, sc_s = (hist_major(b) for b in bufs)
    return (y_p.reshape(nb, seq, d), y_s.reshape(n_dec, dec_seq, d),
            sp_p, sp_s, ss_p, ss_s, sc_p, sc_s)
```

```python
import functools

import jax
import jax.numpy as jnp
from jax import lax
from jax.experimental import pallas as pl
from jax.experimental.pallas import tpu as pltpu

EPS = 1e-6
PAST_LEN = 16384
POOL_WINDOWS = (2, 4, 8, 16)
POOL_GROUP_DIM = 128
POOL_STATE = 15
SCONV_K = 3
CCONV_K = 31

SUBLANES = 8
LANES = 128
BF16_ROWS = 16
HIST_A = 16
HIST_B = 8
HIST_C = 32
VMEM_LIMIT_BYTES = 58 * 1024 * 1024
N_STATES = 3

BF16 = jnp.bfloat16
F32 = jnp.float32
ANY_SPEC = pl.BlockSpec(memory_space=pl.ANY)


def _tiles(n_prompt_rows, n_sample_rows, seq):
    mixer_tm = min(256, seq)
    ffn_row_tiles = max(1, n_prompt_rows // 1024)
    assert n_prompt_rows % ffn_row_tiles == 0 and n_sample_rows % ffn_row_tiles == 0
    assert ((n_prompt_rows + n_sample_rows) // ffn_row_tiles) % BF16_ROWS == 0
    ffn_tf = 512
    return mixer_tm, ffn_row_tiles, ffn_tf


def _cast_split(n_rows, n_outer, n_inner):
    for k in range(n_inner, 0, -1):
        if n_rows % (n_outer * k) == 0 and (n_rows // (n_outer * k)) % BF16_ROWS == 0:
            return k
    raise ValueError(f"cannot split {n_rows} rows over {n_outer}x{n_inner} grid steps")


def _rmsnorm(x, g):
    ms = jnp.mean(x * x, axis=-1, keepdims=True)
    return x * lax.rsqrt(ms + EPS) * g


def _dot(a, b):
    return jnp.dot(a, b, preferred_element_type=F32)


def _sigmoid(x):
    return 1.0 / (1.0 + jnp.exp(-x))


def _layernorm_silu(c, g, b):
    mu = jnp.mean(c, axis=-1, keepdims=True)
    d = c - mu
    var = jnp.mean(d * d, axis=-1, keepdims=True)
    y = d * lax.rsqrt(var + EPS) * g + b
    return y * _sigmoid(y)


def _in_proj_sections(p_width, s_width, c_width):
    o = [0, p_width]
    for w in (s_width, s_width, s_width, c_width, c_width):
        o.append(o[-1] + w)
    names = ("u_a", "b_gate", "c_gate", "x_b", "a_c", "g_c")
    return {n: (o[i], o[i + 1]) for i, n in enumerate(names)}


def _resident(block_shape, index_map):
    return pl.BlockSpec(block_shape, index_map, pipeline_mode=pl.Buffered(1))


def _prompt_mixer_kernel(x_ref, nm_ref, win_ref, pw_ref, ps_ref, sw_ref, cw_ref, cb_ref,
                         lg_ref, lb_ref, wout_ref, wg_ref, wu_ref, wd_ref,
                         xo_ref, nsp_ref, nss_ref, nsc_ref, wgo_ref, wuo_ref, wdo_ref,
                         ea_ref, eb_ref, ec_ref, xs_ref, bg_ref, cc_ref, mix_ref,
                         *, tm, sec):
    t = pl.program_id(1)
    p_width = ea_ref.shape[1]
    s_width = eb_ref.shape[1]
    c_width = ec_ref.shape[1]

    wgo_ref[...] = wg_ref[...].astype(BF16)
    wuo_ref[...] = wu_ref[...].astype(BF16)
    wdo_ref[...] = wd_ref[...].astype(BF16)

    @pl.when(t == 0)
    def _():
        ea_ref[0:HIST_A, :] = jnp.zeros((HIST_A, p_width), F32)
        eb_ref[0:HIST_B, :] = jnp.zeros((HIST_B, s_width), F32)
        ec_ref[0:HIST_C, :] = jnp.zeros((HIST_C, c_width), F32)

    x = x_ref[...]
    h = _rmsnorm(x, nm_ref[...]).astype(BF16)

    def proj(name):
        lo, hi = sec[name]
        return _dot(h, win_ref[:, lo:hi])

    ea_ref[HIST_A:HIST_A + tm, :] = proj("u_a")
    bg_ref[...] = proj("b_gate")
    eb_ref[HIST_B:HIST_B + tm, :] = proj("c_gate") * proj("x_b")
    ec_ref[HIST_C:HIST_C + tm, :] = proj("a_c") * _sigmoid(proj("g_c"))

    pos = t * tm + lax.broadcasted_iota(jnp.int32, (tm, POOL_GROUP_DIM), 0)
    for g, win in enumerate(POOL_WINDOWS):
        lo, hi = g * POOL_GROUP_DIM, (g + 1) * POOL_GROUP_DIM
        u = ea_ref[HIST_A:HIST_A + tm, lo:hi]
        s = u
        for j in range(1, win):
            s = s + ea_ref[HIST_A - j:HIST_A - j + tm, lo:hi]
        cnt = jnp.minimum(pos + 1, win).astype(F32)
        pooled = s / cnt - u
        mixed = _dot(pooled.astype(BF16), pw_ref[g])
        mix_ref[:, lo:hi] = (mixed * ps_ref[:, lo:hi]).astype(BF16)

    conv_b = sw_ref[SCONV_K - 1:SCONV_K, :] * eb_ref[HIST_B:HIST_B + tm, :]
    for j in range(SCONV_K - 1):
        off = HIST_B - (SCONV_K - 1) + j
        conv_b = conv_b + sw_ref[j:j + 1, :] * eb_ref[off:off + tm, :]
    mix_ref[:, p_width:p_width + s_width] = (bg_ref[...] * conv_b).astype(BF16)

    tap0 = HIST_C - (CCONV_K - 1)
    span = tm + HIST_C - SUBLANES
    for c in range(c_width // LANES):
        cols = slice(c * LANES, (c + 1) * LANES)
        for r in range(1, SUBLANES):
            xs_ref[r, 0:span, :] = ec_ref[r:r + span, cols]
        wk = [jnp.broadcast_to(cw_ref[k:k + 1, cols], (SUBLANES, LANES))
              for k in range(CCONV_K)]

        def chunk(i, carry, cols=cols, wk=wk):
            base = pl.multiple_of(i * SUBLANES, SUBLANES)
            acc = [jnp.zeros((SUBLANES, LANES), F32) for _ in range(2)]
            for k in range(CCONV_K):
                q, r = divmod(tap0 + k, SUBLANES)
                if r == 0:
                    src = ec_ref[pl.ds(base + q * SUBLANES, SUBLANES), cols]
                else:
                    src = xs_ref[r, pl.ds(base + q * SUBLANES, SUBLANES), :]
                acc[k % 2] = acc[k % 2] + wk[k] * src
            cc_ref[pl.ds(base, SUBLANES), cols] = acc[0] + acc[1]
            return carry

        lax.fori_loop(0, tm // SUBLANES, chunk, 0, unroll=4)

    y_c = _layernorm_silu(cc_ref[...] + cb_ref[...], lg_ref[...], lb_ref[...])
    mix_ref[:, p_width + s_width:] = y_c.astype(BF16)

    xo_ref[...] = x + _dot(mix_ref[...], wout_ref[...])

    @pl.when(t == pl.num_programs(1) - 1)
    def _():
        nsp_ref[0] = ea_ref[HIST_A + tm - POOL_STATE:HIST_A + tm, :]
        nss_ref[0] = eb_ref[HIST_B + tm - (SCONV_K - 1):HIST_B + tm, :]
        nsc_ref[0] = ec_ref[HIST_C + tm - (CCONV_K - 1):HIST_C + tm, :]

    ea_ref[0:HIST_A, :] = ea_ref[tm:tm + HIST_A, :]
    eb_ref[0:HIST_B, :] = eb_ref[tm:tm + HIST_B, :]
    ec_ref[0:HIST_C, :] = ec_ref[tm:tm + HIST_C, :]


def _prompt_mixer(x, n_rows_out, nb, seq, layer, p, win_bf, wout_bf, ffn_w, tm):
    d = x.shape[1]
    d_in = win_bf.shape[1]
    p_width = p["pool_scale"].shape[2]
    s_width = p["sconv_w"].shape[2]
    c_width = p["cconv_w"].shape[2]
    sec = _in_proj_sections(p_width, s_width, c_width)
    n_t = seq // tm
    n_steps = nb * n_t
    lsel3 = lambda b, t: (layer, 0, 0)
    lsel4 = lambda b, t: (layer, 0, 0, 0)
    const2 = lambda b, t: (0, 0)
    step = lambda b, t: b * n_t + t
    n_groups = len(POOL_WINDOWS)
    in_specs = [
        pl.BlockSpec((tm, d), lambda b, t: (step(b, t), 0)),
        _resident((None, 1, d), lsel3),
        _resident((d, d_in), const2),
        _resident((None, n_groups, POOL_GROUP_DIM, POOL_GROUP_DIM), lsel4),
        _resident((None, 1, p_width), lsel3),
        _resident((None, SCONV_K, s_width), lsel3),
        _resident((None, CCONV_K, c_width), lsel3),
        _resident((None, 1, c_width), lsel3),
        _resident((None, 1, c_width), lsel3),
        _resident((None, 1, c_width), lsel3),
        _resident((d, d), const2),
    ]
    out_shape = [
        jax.ShapeDtypeStruct((n_rows_out, d), F32),
        jax.ShapeDtypeStruct((nb, POOL_STATE, p_width), F32),
        jax.ShapeDtypeStruct((nb, SCONV_K - 1, s_width), F32),
        jax.ShapeDtypeStruct((nb, CCONV_K - 1, c_width), F32),
    ]
    out_specs = [
        pl.BlockSpec((tm, d), lambda b, t: (step(b, t), 0)),
        pl.BlockSpec((1, POOL_STATE, p_width), lambda b, t: (b, 0, 0)),
        pl.BlockSpec((1, SCONV_K - 1, s_width), lambda b, t: (b, 0, 0)),
        pl.BlockSpec((1, CCONV_K - 1, c_width), lambda b, t: (b, 0, 0)),
    ]
    for w in ffn_w:
        _, rows, cols = w.shape
        blk = rows // n_steps
        assert blk * n_steps == rows and blk % BF16_ROWS == 0
        in_specs.append(pl.BlockSpec((None, blk, cols), lambda b, t: (layer, step(b, t), 0)))
        out_specs.append(pl.BlockSpec((blk, cols), lambda b, t: (step(b, t), 0)))
        out_shape.append(jax.ShapeDtypeStruct((rows, cols), BF16))
    scratch = [
        pltpu.VMEM((HIST_A + tm, p_width), F32),
        pltpu.VMEM((HIST_B + tm, s_width), F32),
        pltpu.VMEM((HIST_C + tm, c_width), F32),
        pltpu.VMEM((SUBLANES, tm + HIST_C - SUBLANES, LANES), F32),
        pltpu.VMEM((tm, s_width), F32),
        pltpu.VMEM((tm, c_width), F32),
        pltpu.VMEM((tm, d), BF16),
    ]
    return pl.pallas_call(
        functools.partial(_prompt_mixer_kernel, tm=tm, sec=sec),
        grid=(nb, n_t),
        in_specs=in_specs,
        out_specs=out_specs,
        out_shape=out_shape,
        scratch_shapes=scratch,
        compiler_params=pltpu.CompilerParams(
            dimension_semantics=("parallel", "arbitrary"),
            vmem_limit_bytes=VMEM_LIMIT_BYTES),
        name=f"prompt_mixer_l{layer}",
    )(x, p["norm_mix"], win_bf, p["pool_w"], p["pool_scale"], p["sconv_w"],
      p["cconv_w"], p["cconv_b"], p["cconv_ln_g"], p["cconv_ln_b"], wout_bf, *ffn_w)


def _sample_mixer_kernel(x_ref, stp_ref, sts_ref, stc_ref, xbuf_in, nm_ref, win_ref,
                         pw_ref, ps_ref, sw_ref, cw_ref, cb_ref, lg_ref, lb_ref, wout_ref,
                         xo_ref, nu_ref, nv_ref, nglu_ref, mix_ref, *, sec):
    del xbuf_in
    p_width = nu_ref.shape[1]
    s_width = nv_ref.shape[1]
    x = x_ref[...]
    h = _rmsnorm(x, nm_ref[...]).astype(BF16)

    def proj(name):
        lo, hi = sec[name]
        return _dot(h, win_ref[:, lo:hi])

    u = proj("u_a")
    nu_ref[...] = u
    for g, win in enumerate(POOL_WINDOWS):
        lo, hi = g * POOL_GROUP_DIM, (g + 1) * POOL_GROUP_DIM
        s = u[:, lo:hi]
        for j in range(1, win):
            s = s + stp_ref[POOL_STATE - j, :, lo:hi]
        cnt = float(min(PAST_LEN + 1, win))
        pooled = s / cnt - u[:, lo:hi]
        mixed = _dot(pooled.astype(BF16), pw_ref[g])
        mix_ref[:, lo:hi] = (mixed * ps_ref[:, lo:hi]).astype(BF16)

    v = proj("c_gate") * proj("x_b")
    nv_ref[...] = v
    conv_b = sw_ref[SCONV_K - 1:SCONV_K, :] * v
    for j in range(SCONV_K - 1):
        conv_b = conv_b + sw_ref[j:j + 1, :] * sts_ref[j]
    mix_ref[:, p_width:p_width + s_width] = (proj("b_gate") * conv_b).astype(BF16)

    glu = proj("a_c") * _sigmoid(proj("g_c"))
    nglu_ref[...] = glu
    conv_c = cw_ref[CCONV_K - 1:CCONV_K, :] * glu
    for k in range(CCONV_K - 1):
        conv_c = conv_c + cw_ref[k:k + 1, :] * stc_ref[k]
    y_c = _layernorm_silu(conv_c + cb_ref[...], lg_ref[...], lb_ref[...])
    mix_ref[:, p_width + s_width:] = y_c.astype(BF16)

    xo_ref[...] = x + _dot(mix_ref[...], wout_ref[...])


def _sample_mixer(x, x_block, xbuf, states, layer, p, win_bf, wout_bf):
    d = x.shape[1]
    _, k_pool, n, p_width = states[0].shape
    s_width = states[1].shape[3]
    c_width = states[2].shape[3]
    d_in = win_bf.shape[1]
    sec = _in_proj_sections(p_width, s_width, c_width)
    lsel3 = lambda i: (layer, 0, 0)
    lsel4 = lambda i: (layer, 0, 0, 0)
    const2 = lambda i: (0, 0)
    n_groups = len(POOL_WINDOWS)
    assert xbuf.shape[0] % n == 0
    out_block = xbuf.shape[0] // n - 1
    in_specs = [_resident((n, d), lambda i: (x_block, 0))] + [
        _resident((None,) + s.shape[1:], lsel4) for s in states
    ] + [ANY_SPEC] + [
        _resident((None, 1, d), lsel3),
        _resident((d, d_in), const2),
        _resident((None, n_groups, POOL_GROUP_DIM, POOL_GROUP_DIM), lsel4),
        _resident((None, 1, p_width), lsel3),
        _resident((None, SCONV_K, s_width), lsel3),
        _resident((None, CCONV_K, c_width), lsel3),
        _resident((None, 1, c_width), lsel3),
        _resident((None, 1, c_width), lsel3),
        _resident((None, 1, c_width), lsel3),
        _resident((d, d), const2),
    ]
    out_shape = [jax.ShapeDtypeStruct(xbuf.shape, xbuf.dtype)] + [
        jax.ShapeDtypeStruct((n, w), F32) for w in (p_width, s_width, c_width)]
    out_specs = [pl.BlockSpec((n, d), lambda i: (out_block, 0))] + [
        pl.BlockSpec((n, w), const2) for w in (p_width, s_width, c_width)]
    return pl.pallas_call(
        functools.partial(_sample_mixer_kernel, sec=sec),
        grid=(1,),
        in_specs=in_specs,
        out_specs=out_specs,
        out_shape=out_shape,
        scratch_shapes=[pltpu.VMEM((n, d), BF16)],
        input_output_aliases={1 + N_STATES: 0},
        compiler_params=pltpu.CompilerParams(
            dimension_semantics=("arbitrary",),
            vmem_limit_bytes=VMEM_LIMIT_BYTES),
        name=f"sample_mixer_l{layer}",
    )(x, *states, xbuf, p["norm_mix"], win_bf, p["pool_w"], p["pool_scale"],
      p["sconv_w"], p["cconv_w"], p["cconv_b"], p["cconv_ln_g"], p["cconv_ln_b"],
      wout_bf)


def _state_update_kernel(*refs):
    for st_ref, new_ref, out_ref in zip(refs[:N_STATES], refs[N_STATES:2 * N_STATES],
                                        refs[2 * N_STATES:]):
        keep = st_ref.shape[0] - 1
        out_ref[0:keep] = st_ref[1:keep + 1]
        out_ref[keep] = new_ref[...]


def _state_update(states, new_rows, chunk):
    depth, _, n, _ = states[0].shape
    st_specs = [pl.BlockSpec((None, s.shape[1], chunk, s.shape[3]), lambda l, c: (l, 0, c, 0))
                for s in states]
    new_specs = [pl.BlockSpec((None, chunk, r.shape[2]), lambda l, c: (l, c, 0))
                 for r in new_rows]
    return pl.pallas_call(
        _state_update_kernel,
        grid=(depth, n // chunk),
        in_specs=st_specs + new_specs,
        out_specs=st_specs,
        out_shape=[jax.ShapeDtypeStruct(s.shape, s.dtype) for s in states],
        compiler_params=pltpu.CompilerParams(
            dimension_semantics=("parallel", "parallel"),
            vmem_limit_bytes=VMEM_LIMIT_BYTES),
        name="sample_state_update",
    )(*states, *new_rows)


def _ffn_kernel(*refs, apply_final_norm, n_cast):
    x_ref, nf_ref, wg_ref, wu_ref, wd_ref, nfin_ref = refs[:6]
    cast_in = refs[6:6 + n_cast]
    o_ref = refs[6 + n_cast]
    cast_out = refs[7 + n_cast:7 + 2 * n_cast]
    h2_ref = refs[7 + 2 * n_cast]
    j = pl.program_id(1)

    for src, dst in zip(cast_in, cast_out):
        dst[...] = src[...].astype(BF16)

    @pl.when(j == 0)
    def _():
        x = x_ref[...]
        h2_ref[...] = _rmsnorm(x, nf_ref[...]).astype(BF16)
        o_ref[...] = x

    h2 = h2_ref[...]
    gate = _dot(h2, wg_ref[...])
    up = _dot(h2, wu_ref[...])
    f = (gate * _sigmoid(gate) * up).astype(BF16)
    o_ref[...] += _dot(f, wd_ref[...])

    if apply_final_norm:
        @pl.when(j == pl.num_programs(1) - 1)
        def _():
            o_ref[...] = _rmsnorm(o_ref[...], nfin_ref[...])


def _ffn(x, first_block, n_i, tm, layer, p, wg_bf, wu_bf, wd_bf, tf, apply_final_norm,
         cast_next=()):
    d = x.shape[1]
    f_width = wg_bf.shape[1]
    n_j = f_width // tf
    in_specs = [
        pl.BlockSpec((tm, d), lambda i, j: (first_block + i, 0)),
        pl.BlockSpec((None, 1, d), lambda i, j: (layer, 0, 0)),
        pl.BlockSpec((d, tf), lambda i, j: (0, j)),
        pl.BlockSpec((d, tf), lambda i, j: (0, j)),
        pl.BlockSpec((tf, d), lambda i, j: (j, 0)),
        pl.BlockSpec((1, d), lambda i, j: (0, 0)),
    ]
    out_specs = [pl.BlockSpec((tm, d), lambda i, j: (i, 0))]
    out_shape = [jax.ShapeDtypeStruct((n_i * tm, d), F32)]
    for w in cast_next:
        _, rows, cols = w.shape
        k = _cast_split(rows, n_i, n_j)
        blk = rows // (n_i * k)
        step = lambda i, j, k=k: i * k + jnp.minimum(j, k - 1)
        in_specs.append(pl.BlockSpec((None, blk, cols),
                                     lambda i, j, step=step: (layer + 1, step(i, j), 0)))
        out_specs.append(pl.BlockSpec((blk, cols), lambda i, j, step=step: (step(i, j), 0)))
        out_shape.append(jax.ShapeDtypeStruct((rows, cols), BF16))
    return pl.pallas_call(
        functools.partial(_ffn_kernel, apply_final_norm=apply_final_norm,
                          n_cast=len(cast_next)),
        grid=(n_i, n_j),
        in_specs=in_specs,
        out_specs=out_specs,
        out_shape=out_shape,
        scratch_shapes=[pltpu.VMEM((tm, d), BF16)],
        compiler_params=pltpu.CompilerParams(
            dimension_semantics=("parallel", "arbitrary"),
            vmem_limit_bytes=VMEM_LIMIT_BYTES),
        name=f"ffn_l{layer}_m{n_i * tm}",
    )(x, p["norm_ffn"], wg_bf, wu_bf, wd_bf, p["norm_final"], *cast_next)


def kernel(x_prompt, x_sample, state_pool, state_sconv, state_cconv, norm_mix, w_in,
           pool_w, pool_scale, sconv_w, cconv_w, cconv_b, cconv_ln_g, cconv_ln_b,
           w_out, norm_ffn, w_gate, w_up, w_down, norm_final):
    depth = w_in.shape[0]
    nb, seq, d = x_prompt.shape
    n_dec, dec_seq, _ = x_sample.shape
    assert dec_seq == 1
    n_prompt = nb * seq
    n_rows = n_prompt + n_dec
    assert n_prompt % n_dec == 0
    row = lambda a: a[:, None, :]
    p = dict(
        norm_mix=row(norm_mix), pool_w=pool_w.astype(BF16), pool_scale=row(pool_scale),
        sconv_w=sconv_w, cconv_w=cconv_w, cconv_b=row(cconv_b),
        cconv_ln_g=row(cconv_ln_g), cconv_ln_b=row(cconv_ln_b), norm_ffn=row(norm_ffn),
        norm_final=norm_final[None, :])

    mixer_tm, ffn_tiles, ffn_tf = _tiles(n_prompt, n_dec, seq)

    win_bf, wout_bf = w_in[0].astype(BF16), w_out[0].astype(BF16)

    hist_major = lambda s: jnp.transpose(s, (0, 2, 1, 3))
    states = tuple(hist_major(s) for s in (state_pool, state_sconv, state_cconv))

    x_p = x_prompt.reshape(n_prompt, d)
    x_s, x_s_block = x_sample.reshape(n_dec, d), 0
    prompt_states = ([], [], [])
    sample_rows = ([], [], [])
    for l in range(depth):
        last = l == depth - 1
        xbuf, nsp, nss, nsc, wg_bf, wu_bf, wd_bf = _prompt_mixer(
            x_p, n_rows, nb, seq, l, p, win_bf, wout_bf, (w_gate, w_up, w_down), mixer_tm)
        for acc, s in zip(prompt_states, (nsp, nss, nsc)):
            acc.append(s)
        xbuf, *new_rows = _sample_mixer(x_s, x_s_block, xbuf, states, l, p, win_bf, wout_bf)
        for acc, r in zip(sample_rows, new_rows):
            acc.append(r)
        if not last:
            outs = _ffn(xbuf, 0, ffn_tiles, n_rows // ffn_tiles, l, p, wg_bf, wu_bf, wd_bf,
                        ffn_tf, apply_final_norm=False, cast_next=(w_in, w_out))
            x_p = x_s = outs[0]
            x_s_block = n_prompt // n_dec
            win_bf, wout_bf = outs[1:]
        else:
            y_p = _ffn(xbuf, 0, ffn_tiles, n_prompt // ffn_tiles, l, p, wg_bf, wu_bf, wd_bf,
                       ffn_tf, apply_final_norm=True)[0]
            y_s = _ffn(xbuf, n_prompt // n_dec, 1, n_dec, l, p, wg_bf, wu_bf, wd_bf,
                       ffn_tf, apply_final_norm=True)[0]

    sp_p, ss_p, sc_p = (jnp.stack(s) for s in prompt_states)
    new_states = _state_update(states, [jnp.stack(r) for r in sample_rows],
                               chunk=min(32, n_dec))
    sp_s, ss_s, sc_s = (hist_major(s) for s in new_states)
    return (y_p.reshape(nb, seq, d), y_s.reshape(n_dec, dec_seq, d),
            sp_p, sp_s, ss_p, ss_s, sc_p, sc_s)
```

```python
import functools

import jax
import jax.numpy as jnp
from jax import lax
from jax.experimental import pallas as pl
from jax.experimental.pallas import tpu as pltpu

EPS = 1e-6
PAST_LEN = 16384
POOL_WINDOWS = (2, 4, 8, 16)
POOL_GROUP_DIM = 128
POOL_STATE = 15
SCONV_K = 3
CCONV_K = 31

SUBLANES = 8
LANES = 128
BF16_ROWS = 16
HIST_A = 16
HIST_B = 8
HIST_C = 32
VMEM_LIMIT_BYTES = 58 * 1024 * 1024
N_STATES = 3

BF16 = jnp.bfloat16
F32 = jnp.float32
ANY_SPEC = pl.BlockSpec(memory_space=pl.ANY)


def _tiles(n_prompt_rows, n_sample_rows, seq):
    mixer_tm = min(256, seq)
    ffn_row_tiles = max(1, n_prompt_rows // 1024)
    assert n_prompt_rows % ffn_row_tiles == 0 and n_sample_rows % ffn_row_tiles == 0
    assert ((n_prompt_rows + n_sample_rows) // ffn_row_tiles) % BF16_ROWS == 0
    ffn_tf = 512
    return mixer_tm, ffn_row_tiles, ffn_tf


def _cast_split(n_rows, n_outer, n_inner):
    for k in range(n_inner, 0, -1):
        if n_rows % (n_outer * k) == 0 and (n_rows // (n_outer * k)) % BF16_ROWS == 0:
            return k
    raise ValueError(f"cannot split {n_rows} rows over {n_outer}x{n_inner} grid steps")


def _rmsnorm(x, g):
    ms = jnp.mean(x * x, axis=-1, keepdims=True)
    return x * lax.rsqrt(ms + EPS) * g


def _dot(a, b):
    return jnp.dot(a, b, preferred_element_type=F32)


def _sigmoid(x):
    return 1.0 / (1.0 + jnp.exp(-x))


def _layernorm_silu(c, g, b):
    mu = jnp.mean(c, axis=-1, keepdims=True)
    d = c - mu
    var = jnp.mean(d * d, axis=-1, keepdims=True)
    y = d * lax.rsqrt(var + EPS) * g + b
    return y * _sigmoid(y)


def _in_proj_sections(p_width, s_width, c_width):
    o = [0, p_width]
    for w in (s_width, s_width, s_width, c_width, c_width):
        o.append(o[-1] + w)
    names = ("u_a", "b_gate", "c_gate", "x_b", "a_c", "g_c")
    return {n: (o[i], o[i + 1]) for i, n in enumerate(names)}


def _resident(block_shape, index_map):
    return pl.BlockSpec(block_shape, index_map, pipeline_mode=pl.Buffered(1))


def _prompt_mixer_kernel(x_ref, nm_ref, win_ref, pw_ref, ps_ref, sw_ref, cw_ref, cb_ref,
                         lg_ref, lb_ref, wout_ref, wg_ref, wu_ref, wd_ref,
                         xo_ref, nsp_ref, nss_ref, nsc_ref, wgo_ref, wuo_ref, wdo_ref,
                         ea_ref, eb_ref, ec_ref, xs_ref, bg_ref, cc_ref, mix_ref,
                         *, tm, sec):
    t = pl.program_id(1)
    p_width = ea_ref.shape[1]
    s_width = eb_ref.shape[1]
    c_width = ec_ref.shape[1]

    wgo_ref[...] = wg_ref[...].astype(BF16)
    wuo_ref[...] = wu_ref[...].astype(BF16)
    wdo_ref[...] = wd_ref[...].astype(BF16)

    @pl.when(t == 0)
    def _():
        ea_ref[0:HIST_A, :] = jnp.zeros((HIST_A, p_width), F32)
        eb_ref[0:HIST_B, :] = jnp.zeros((HIST_B, s_width), F32)
        ec_ref[0:HIST_C, :] = jnp.zeros((HIST_C, c_width), F32)

    x = x_ref[...]
    h = _rmsnorm(x, nm_ref[...]).astype(BF16)

    def proj(name):
        lo, hi = sec[name]
        return _dot(h, win_ref[:, lo:hi])

    ea_ref[HIST_A:HIST_A + tm, :] = proj("u_a")
    bg_ref[...] = proj("b_gate")
    eb_ref[HIST_B:HIST_B + tm, :] = proj("c_gate") * proj("x_b")
    ec_ref[HIST_C:HIST_C + tm, :] = proj("a_c") * _sigmoid(proj("g_c"))

    pos = t * tm + lax.broadcasted_iota(jnp.int32, (tm, POOL_GROUP_DIM), 0)
    for g, win in enumerate(POOL_WINDOWS):
        lo, hi = g * POOL_GROUP_DIM, (g + 1) * POOL_GROUP_DIM
        u = ea_ref[HIST_A:HIST_A + tm, lo:hi]
        s = u
        for j in range(1, win):
            s = s + ea_ref[HIST_A - j:HIST_A - j + tm, lo:hi]
        cnt = jnp.minimum(pos + 1, win).astype(F32)
        pooled = s / cnt - u
        mixed = _dot(pooled.astype(BF16), pw_ref[g])
        mix_ref[:, lo:hi] = (mixed * ps_ref[:, lo:hi]).astype(BF16)

    conv_b = sw_ref[SCONV_K - 1:SCONV_K, :] * eb_ref[HIST_B:HIST_B + tm, :]
    for j in range(SCONV_K - 1):
        off = HIST_B - (SCONV_K - 1) + j
        conv_b = conv_b + sw_ref[j:j + 1, :] * eb_ref[off:off + tm, :]
    mix_ref[:, p_width:p_width + s_width] = (bg_ref[...] * conv_b).astype(BF16)

    tap0 = HIST_C - (CCONV_K - 1)
    span = tm + HIST_C - SUBLANES
    for c in range(c_width // LANES):
        cols = slice(c * LANES, (c + 1) * LANES)
        for r in range(1, SUBLANES):
            xs_ref[r, 0:span, :] = ec_ref[r:r + span, cols]
        wk = [jnp.broadcast_to(cw_ref[k:k + 1, cols], (SUBLANES, LANES))
              for k in range(CCONV_K)]

        def chunk(i, carry, cols=cols, wk=wk):
            base = pl.multiple_of(i * SUBLANES, SUBLANES)
            acc = [jnp.zeros((SUBLANES, LANES), F32) for _ in range(2)]
            for k in range(CCONV_K):
                q, r = divmod(tap0 + k, SUBLANES)
                if r == 0:
                    src = ec_ref[pl.ds(base + q * SUBLANES, SUBLANES), cols]
                else:
                    src = xs_ref[r, pl.ds(base + q * SUBLANES, SUBLANES), :]
                acc[k % 2] = acc[k % 2] + wk[k] * src
            cc_ref[pl.ds(base, SUBLANES), cols] = acc[0] + acc[1]
            return carry

        lax.fori_loop(0, tm // SUBLANES, chunk, 0, unroll=4)

    y_c = _layernorm_silu(cc_ref[...] + cb_ref[...], lg_ref[...], lb_ref[...])
    mix_ref[:, p_width + s_width:] = y_c.astype(BF16)

    xo_ref[...] = x + _dot(mix_ref[...], wout_ref[...])

    @pl.when(t == pl.num_programs(1) - 1)
    def _():
        nsp_ref[0] = ea_ref[HIST_A + tm - POOL_STATE:HIST_A + tm, :]
        nss_ref[0] = eb_ref[HIST_B + tm - (SCONV_K - 1):HIST_B + tm, :]
        nsc_ref[0] = ec_ref[HIST_C + tm - (CCONV_K - 1):HIST_C + tm, :]

    ea_ref[0:HIST_A, :] = ea_ref[tm:tm + HIST_A, :]
    eb_ref[0:HIST_B, :] = eb_ref[tm:tm + HIST_B, :]
    ec_ref[0:HIST_C, :] = ec_ref[tm:tm + HIST_C, :]


def _prompt_mixer(x, n_rows_out, nb, seq, layer, p, win_bf, wout_bf, ffn_w, tm):
    d = x.shape[1]
    d_in = win_bf.shape[1]
    p_width = p["pool_scale"].shape[2]
    s_width = p["sconv_w"].shape[2]
    c_width = p["cconv_w"].shape[2]
    sec = _in_proj_sections(p_width, s_width, c_width)
    n_t = seq // tm
    n_steps = nb * n_t
    lsel3 = lambda b, t: (layer, 0, 0)
    lsel4 = lambda b, t: (layer, 0, 0, 0)
    const2 = lambda b, t: (0, 0)
    step = lambda b, t: b * n_t + t
    n_groups = len(POOL_WINDOWS)
    in_specs = [
        pl.BlockSpec((tm, d), lambda b, t: (step(b, t), 0)),
        _resident((None, 1, d), lsel3),
        _resident((d, d_in), const2),
        _resident((None, n_groups, POOL_GROUP_DIM, POOL_GROUP_DIM), lsel4),
        _resident((None, 1, p_width), lsel3),
        _resident((None, SCONV_K, s_width), lsel3),
        _resident((None, CCONV_K, c_width), lsel3),
        _resident((None, 1, c_width), lsel3),
        _resident((None, 1, c_width), lsel3),
        _resident((None, 1, c_width), lsel3),
        _resident((d, d), const2),
    ]
    out_shape = [
        jax.ShapeDtypeStruct((n_rows_out, d), F32),
        jax.ShapeDtypeStruct((nb, POOL_STATE, p_width), F32),
        jax.ShapeDtypeStruct((nb, SCONV_K - 1, s_width), F32),
        jax.ShapeDtypeStruct((nb, CCONV_K - 1, c_width), F32),
    ]
    out_specs = [
        pl.BlockSpec((tm, d), lambda b, t: (step(b, t), 0)),
        pl.BlockSpec((1, POOL_STATE, p_width), lambda b, t: (b, 0, 0)),
        pl.BlockSpec((1, SCONV_K - 1, s_width), lambda b, t: (b, 0, 0)),
        pl.BlockSpec((1, CCONV_K - 1, c_width), lambda b, t: (b, 0, 0)),
    ]
    for w in ffn_w:
        _, rows, cols = w.shape
        blk = rows // n_steps
        assert blk * n_steps == rows and blk % BF16_ROWS == 0
        in_specs.append(pl.BlockSpec((None, blk, cols), lambda b, t: (layer, step(b, t), 0)))
        out_specs.append(pl.BlockSpec((blk, cols), lambda b, t: (step(b, t), 0)))
        out_shape.append(jax.ShapeDtypeStruct((rows, cols), BF16))
    scratch = [
        pltpu.VMEM((HIST_A + tm, p_width), F32),
        pltpu.VMEM((HIST_B + tm, s_width), F32),
        pltpu.VMEM((HIST_C + tm, c_width), F32),
        pltpu.VMEM((SUBLANES, tm + HIST_C - SUBLANES, LANES), F32),
        pltpu.VMEM((tm, s_width), F32),
        pltpu.VMEM((tm, c_width), F32),
        pltpu.VMEM((tm, d), BF16),
    ]
    return pl.pallas_call(
        functools.partial(_prompt_mixer_kernel, tm=tm, sec=sec),
        grid=(nb, n_t),
        in_specs=in_specs,
        out_specs=out_specs,
        out_shape=out_shape,
        scratch_shapes=scratch,
        compiler_params=pltpu.CompilerParams(
            dimension_semantics=("parallel", "arbitrary"),
            vmem_limit_bytes=VMEM_LIMIT_BYTES),
        name=f"prompt_mixer_l{layer}",
    )(x, p["norm_mix"], win_bf, p["pool_w"], p["pool_scale"], p["sconv_w"],
      p["cconv_w"], p["cconv_b"], p["cconv_ln_g"], p["cconv_ln_b"], wout_bf, *ffn_w)


def _sample_mixer_kernel(x_ref, stp_ref, sts_ref, stc_ref, xbuf_in, nm_ref, win_ref,
                         pw_ref, ps_ref, sw_ref, cw_ref, cb_ref, lg_ref, lb_ref, wout_ref,
                         xo_ref, nu_ref, nv_ref, nglu_ref, mix_ref, *, sec):
    del xbuf_in
    p_width = nu_ref.shape[1]
    s_width = nv_ref.shape[1]
    x = x_ref[...]
    h = _rmsnorm(x, nm_ref[...]).astype(BF16)

    def proj(name):
        lo, hi = sec[name]
        return _dot(h, win_ref[:, lo:hi])

    u = proj("u_a")
    nu_ref[...] = u
    for g, win in enumerate(POOL_WINDOWS):
        lo, hi = g * POOL_GROUP_DIM, (g + 1) * POOL_GROUP_DIM
        s = u[:, lo:hi]
        for j in range(1, win):
            s = s + stp_ref[POOL_STATE - j, :, lo:hi]
        cnt = float(min(PAST_LEN + 1, win))
        pooled = s / cnt - u[:, lo:hi]
        mixed = _dot(pooled.astype(BF16), pw_ref[g])
        mix_ref[:, lo:hi] = (mixed * ps_ref[:, lo:hi]).astype(BF16)

    v = proj("c_gate") * proj("x_b")
    nv_ref[...] = v
    conv_b = sw_ref[SCONV_K - 1:SCONV_K, :] * v
    for j in range(SCONV_K - 1):
        conv_b = conv_b + sw_ref[j:j + 1, :] * sts_ref[j]
    mix_ref[:, p_width:p_width + s_width] = (proj("b_gate") * conv_b).astype(BF16)

    glu = proj("a_c") * _sigmoid(proj("g_c"))
    nglu_ref[...] = glu
    conv_c = cw_ref[CCONV_K - 1:CCONV_K, :] * glu
    for k in range(CCONV_K - 1):
        conv_c = conv_c + cw_ref[k:k + 1, :] * stc_ref[k]
    y_c = _layernorm_silu(conv_c + cb_ref[...], lg_ref[...], lb_ref[...])
    mix_ref[:, p_width + s_width:] = y_c.astype(BF16)

    xo_ref[...] = x + _dot(mix_ref[...], wout_ref[...])


def _sample_mixer(x, x_block, xbuf, states, layer, p, win_bf, wout_bf):
    d = x.shape[1]
    _, k_pool, n, p_width = states[0].shape
    s_width = states[1].shape[3]
    c_width = states[2].shape[3]
    d_in = win_bf.shape[1]
    sec = _in_proj_sections(p_width, s_width, c_width)
    lsel3 = lambda i: (layer, 0, 0)
    lsel4 = lambda i: (layer, 0, 0, 0)
    const2 = lambda i: (0, 0)
    n_groups = len(POOL_WINDOWS)
    assert xbuf.shape[0] % n == 0
    out_block = xbuf.shape[0] // n - 1
    in_specs = [_resident((n, d), lambda i: (x_block, 0))] + [
        _resident((None,) + s.shape[1:], lsel4) for s in states
    ] + [ANY_SPEC] + [
        _resident((None, 1, d), lsel3),
        _resident((d, d_in), const2),
        _resident((None, n_groups, POOL_GROUP_DIM, POOL_GROUP_DIM), lsel4),
        _resident((None, 1, p_width), lsel3),
        _resident((None, SCONV_K, s_width), lsel3),
        _resident((None, CCONV_K, c_width), lsel3),
        _resident((None, 1, c_width), lsel3),
        _resident((None, 1, c_width), lsel3),
        _resident((None, 1, c_width), lsel3),
        _resident((d, d), const2),
    ]
    out_shape = [jax.ShapeDtypeStruct(xbuf.shape, xbuf.dtype)] + [
        jax.ShapeDtypeStruct((n, w), F32) for w in (p_width, s_width, c_width)]
    out_specs = [pl.BlockSpec((n, d), lambda i: (out_block, 0))] + [
        pl.BlockSpec((n, w), const2) for w in (p_width, s_width, c_width)]
    return pl.pallas_call(
        functools.partial(_sample_mixer_kernel, sec=sec),
        grid=(1,),
        in_specs=in_specs,
        out_specs=out_specs,
        out_shape=out_shape,
        scratch_shapes=[pltpu.VMEM((n, d), BF16)],
        input_output_aliases={1 + N_STATES: 0},
        compiler_params=pltpu.CompilerParams(
            dimension_semantics=("arbitrary",),
            vmem_limit_bytes=VMEM_LIMIT_BYTES),
        name=f"sample_mixer_l{layer}",
    )(x, *states, xbuf, p["norm_mix"], win_bf, p["pool_w"], p["pool_scale"],
      p["sconv_w"], p["cconv_w"], p["cconv_b"], p["cconv_ln_g"], p["cconv_ln_b"],
      wout_bf)


def _state_update_kernel(*refs):
    for st_ref, new_ref, out_ref in zip(refs[:N_STATES], refs[N_STATES:2 * N_STATES],
                                        refs[2 * N_STATES:]):
        keep = st_ref.shape[0] - 1
        out_ref[0:keep] = st_ref[1:keep + 1]
        out_ref[keep] = new_ref[...]


def _state_update(states, new_rows, chunk):
    depth, _, n, _ = states[0].shape
    st_specs = [pl.BlockSpec((None, s.shape[1], chunk, s.shape[3]), lambda l, c: (l, 0, c, 0))
                for s in states]
    new_specs = [pl.BlockSpec((None, chunk, r.shape[2]), lambda l, c: (l, c, 0))
                 for r in new_rows]
    return pl.pallas_call(
        _state_update_kernel,
        grid=(depth, n // chunk),
        in_specs=st_specs + new_specs,
        out_specs=st_specs,
        out_shape=[jax.ShapeDtypeStruct(s.shape, s.dtype) for s in states],
        compiler_params=pltpu.CompilerParams(
            dimension_semantics=("parallel", "parallel"),
            vmem_limit_bytes=VMEM_LIMIT_BYTES),
        name="sample_state_update",
    )(*states, *new_rows)


def _ffn_kernel(*refs, apply_final_norm, n_cast):
    x_ref, nf_ref, wg_ref, wu_ref, wd_ref, nfin_ref = refs[:6]
    cast_in = refs[6:6 + n_cast]
    o_ref = refs[6 + n_cast]
    cast_out = refs[7 + n_cast:7 + 2 * n_cast]
    h2_ref = refs[7 + 2 * n_cast]
    j = pl.program_id(1)

    for src, dst in zip(cast_in, cast_out):
        dst[...] = src[...].astype(BF16)

    def hidden_chunk():
        h2 = h2_ref[...]
        gate = _dot(h2, wg_ref[...])
        up = _dot(h2, wu_ref[...])
        f = (gate * _sigmoid(gate) * up).astype(BF16)
        return _dot(f, wd_ref[...])

    @pl.when(j == 0)
    def _():
        x = x_ref[...]
        h2_ref[...] = _rmsnorm(x, nf_ref[...]).astype(BF16)
        o_ref[...] = x + hidden_chunk()

    @pl.when(j > 0)
    def _():
        o_ref[...] += hidden_chunk()

    if apply_final_norm:
        @pl.when(j == pl.num_programs(1) - 1)
        def _():
            o_ref[...] = _rmsnorm(o_ref[...], nfin_ref[...])


def _ffn(x, first_block, n_i, tm, layer, p, wg_bf, wu_bf, wd_bf, tf, apply_final_norm,
         cast_next=()):
    d = x.shape[1]
    f_width = wg_bf.shape[1]
    n_j = f_width // tf
    in_specs = [
        pl.BlockSpec((tm, d), lambda i, j: (first_block + i, 0)),
        pl.BlockSpec((None, 1, d), lambda i, j: (layer, 0, 0)),
        pl.BlockSpec((d, tf), lambda i, j: (0, j)),
        pl.BlockSpec((d, tf), lambda i, j: (0, j)),
        pl.BlockSpec((tf, d), lambda i, j: (j, 0)),
        pl.BlockSpec((1, d), lambda i, j: (0, 0)),
    ]
    out_specs = [pl.BlockSpec((tm, d), lambda i, j: (i, 0))]
    out_shape = [jax.ShapeDtypeStruct((n_i * tm, d), F32)]
    for w in cast_next:
        _, rows, cols = w.shape
        k = _cast_split(rows, n_i, n_j)
        blk = rows // (n_i * k)
        step = lambda i, j, k=k: i * k + jnp.minimum(j, k - 1)
        in_specs.append(pl.BlockSpec((None, blk, cols),
                                     lambda i, j, step=step: (layer + 1, step(i, j), 0)))
        out_specs.append(pl.BlockSpec((blk, cols), lambda i, j, step=step: (step(i, j), 0)))
        out_shape.append(jax.ShapeDtypeStruct((rows, cols), BF16))
    return pl.pallas_call(
        functools.partial(_ffn_kernel, apply_final_norm=apply_final_norm,
                          n_cast=len(cast_next)),
        grid=(n_i, n_j),
        in_specs=in_specs,
        out_specs=out_specs,
        out_shape=out_shape,
        scratch_shapes=[pltpu.VMEM((tm, d), BF16)],
        compiler_params=pltpu.CompilerParams(
            dimension_semantics=("parallel", "arbitrary"),
            vmem_limit_bytes=VMEM_LIMIT_BYTES),
        name=f"ffn_l{layer}_m{n_i * tm}",
    )(x, p["norm_ffn"], wg_bf, wu_bf, wd_bf, p["norm_final"], *cast_next)


def kernel(x_prompt, x_sample, state_pool, state_sconv, state_cconv, norm_mix, w_in,
           pool_w, pool_scale, sconv_w, cconv_w, cconv_b, cconv_ln_g, cconv_ln_b,
           w_out, norm_ffn, w_gate, w_up, w_down, norm_final):
    depth = w_in.shape[0]
    nb, seq, d = x_prompt.shape
    n_dec, dec_seq, _ = x_sample.shape
    assert dec_seq == 1
    n_prompt = nb * seq
    n_rows = n_prompt + n_dec
    assert n_prompt % n_dec == 0
    row = lambda a: a[:, None, :]
    p = dict(
        norm_mix=row(norm_mix), pool_w=pool_w.astype(BF16), pool_scale=row(pool_scale),
        sconv_w=sconv_w, cconv_w=cconv_w, cconv_b=row(cconv_b),
        cconv_ln_g=row(cconv_ln_g), cconv_ln_b=row(cconv_ln_b), norm_ffn=row(norm_ffn),
        norm_final=norm_final[None, :])

    mixer_tm, ffn_tiles, ffn_tf = _tiles(n_prompt, n_dec, seq)

    win_bf, wout_bf = w_in[0].astype(BF16), w_out[0].astype(BF16)

    hist_major = lambda s: jnp.transpose(s, (0, 2, 1, 3))
    states = tuple(hist_major(s) for s in (state_pool, state_sconv, state_cconv))

    x_p = x_prompt.reshape(n_prompt, d)
    x_s, x_s_block = x_sample.reshape(n_dec, d), 0
    prompt_states = ([], [], [])
    sample_rows = ([], [], [])
    for l in range(depth):
        last = l == depth - 1
        xbuf, nsp, nss, nsc, wg_bf, wu_bf, wd_bf = _prompt_mixer(
            x_p, n_rows, nb, seq, l, p, win_bf, wout_bf, (w_gate, w_up, w_down), mixer_tm)
        for acc, s in zip(prompt_states, (nsp, nss, nsc)):
            acc.append(s)
        xbuf, *new_rows = _sample_mixer(x_s, x_s_block, xbuf, states, l, p, win_bf, wout_bf)
        for acc, r in zip(sample_rows, new_rows):
            acc.append(r)
        if not last:
            outs = _ffn(xbuf, 0, ffn_tiles, n_rows // ffn_tiles, l, p, wg_bf, wu_bf, wd_bf,
                        ffn_tf, apply_final_norm=False, cast_next=(w_in, w_out))
            x_p = x_s = outs[0]
            x_s_block = n_prompt // n_dec
            win_bf, wout_bf = outs[1:]
        else:
            y_p = _ffn(xbuf, 0, ffn_tiles, n_prompt // ffn_tiles, l, p, wg_bf, wu_bf, wd_bf,
                       ffn_tf, apply_final_norm=True)[0]
            y_s = _ffn(xbuf, n_prompt // n_dec, 1, n_dec, l, p, wg_bf, wu_bf, wd_bf,
                       ffn_tf, apply_final_norm=True)[0]

    sp_p, ss_p, sc_p = (jnp.stack(s) for s in prompt_states)
    new_states = _state_update(states, [jnp.stack(r) for r in sample_rows],
                               chunk=min(32, n_dec))
    sp_s, ss_s, sc_s = (hist_major(s) for s in new_states)
    return (y_p.reshape(nb, seq, d), y_s.reshape(n_dec, dec_seq, d),
            sp_p, sp_s, ss_p, ss_s, sc_p, sc_s)
```

```python
import functools

import jax
import jax.numpy as jnp
from jax import lax
from jax.experimental import pallas as pl
from jax.experimental.pallas import tpu as pltpu

EPS = 1e-6
PAST_LEN = 16384
POOL_WINDOWS = (2, 4, 8, 16)
POOL_GROUP_DIM = 128
POOL_STATE = 15
SCONV_K = 3
CCONV_K = 31

SUBLANES = 8
LANES = 128
BF16_ROWS = 16
HIST_A = 16
HIST_B = 8
HIST_C = 32
CONV_PARTIAL_SUMS = 4
VMEM_LIMIT_BYTES = 58 * 1024 * 1024
N_STATES = 3

BF16 = jnp.bfloat16
F32 = jnp.float32
ANY_SPEC = pl.BlockSpec(memory_space=pl.ANY)


def _tiles(n_prompt_rows, n_sample_rows, seq):
    mixer_tm = min(256, seq)
    ffn_row_tiles = max(1, n_prompt_rows // 1024)
    assert n_prompt_rows % ffn_row_tiles == 0 and n_sample_rows % ffn_row_tiles == 0
    assert ((n_prompt_rows + n_sample_rows) // ffn_row_tiles) % BF16_ROWS == 0
    ffn_tf = 512
    return mixer_tm, ffn_row_tiles, ffn_tf


def _cast_split(n_rows, n_outer, n_inner):
    for k in range(n_inner, 0, -1):
        if n_rows % (n_outer * k) == 0 and (n_rows // (n_outer * k)) % BF16_ROWS == 0:
            return k
    raise ValueError(f"cannot split {n_rows} rows over {n_outer}x{n_inner} grid steps")


def _rmsnorm(x, g):
    ms = jnp.mean(x * x, axis=-1, keepdims=True)
    return x * lax.rsqrt(ms + EPS) * g


def _dot(a, b):
    return jnp.dot(a, b, preferred_element_type=F32)


def _sigmoid(x):
    return 0.5 * jnp.tanh(0.5 * x) + 0.5


def _layernorm_silu(c, g, b):
    mu = jnp.mean(c, axis=-1, keepdims=True)
    d = c - mu
    var = jnp.mean(d * d, axis=-1, keepdims=True)
    y = d * lax.rsqrt(var + EPS) * g + b
    return y * _sigmoid(y)


def _in_proj_sections(p_width, s_width, c_width):
    o = [0, p_width]
    for w in (s_width, s_width, s_width, c_width, c_width):
        o.append(o[-1] + w)
    names = ("u_a", "b_gate", "c_gate", "x_b", "a_c", "g_c")
    return {n: (o[i], o[i + 1]) for i, n in enumerate(names)}


def _resident(block_shape, index_map):
    return pl.BlockSpec(block_shape, index_map, pipeline_mode=pl.Buffered(1))


def _prompt_mixer_kernel(x_ref, nm_ref, win_ref, pw_ref, ps_ref, sw_ref, cw_ref, cb_ref,
                         lg_ref, lb_ref, wout_ref, wg_ref, wu_ref, wd_ref,
                         xo_ref, nsp_ref, nss_ref, nsc_ref, wgo_ref, wuo_ref, wdo_ref,
                         ea_ref, eb_ref, ec_ref, xs_ref, bg_ref, cc_ref, mix_ref,
                         *, tm, sec):
    t = pl.program_id(1)
    p_width = ea_ref.shape[1]
    s_width = eb_ref.shape[1]
    c_width = ec_ref.shape[1]

    @pl.when(t == 0)
    def _():
        ea_ref[0:HIST_A, :] = jnp.zeros((HIST_A, p_width), F32)
        eb_ref[0:HIST_B, :] = jnp.zeros((HIST_B, s_width), F32)
        ec_ref[0:HIST_C, :] = jnp.zeros((HIST_C, c_width), F32)

    x = x_ref[...]
    h = _rmsnorm(x, nm_ref[...]).astype(BF16)

    def proj(name):
        lo, hi = sec[name]
        return _dot(h, win_ref[:, lo:hi])

    ea_ref[HIST_A:HIST_A + tm, :] = proj("u_a")
    wgo_ref[...] = wg_ref[...].astype(BF16)
    bg_ref[...] = proj("b_gate")
    wuo_ref[...] = wu_ref[...].astype(BF16)
    eb_ref[HIST_B:HIST_B + tm, :] = proj("c_gate") * proj("x_b")
    wdo_ref[...] = wd_ref[...].astype(BF16)
    ec_ref[HIST_C:HIST_C + tm, :] = proj("a_c") * _sigmoid(proj("g_c"))

    pos = t * tm + lax.broadcasted_iota(jnp.int32, (tm, POOL_GROUP_DIM), 0)
    for g, win in enumerate(POOL_WINDOWS):
        lo, hi = g * POOL_GROUP_DIM, (g + 1) * POOL_GROUP_DIM
        u = ea_ref[HIST_A:HIST_A + tm, lo:hi]
        s = u
        for j in range(1, win):
            s = s + ea_ref[HIST_A - j:HIST_A - j + tm, lo:hi]
        cnt = jnp.minimum(pos + 1, win).astype(F32)
        pooled = s / cnt - u
        mixed = _dot(pooled.astype(BF16), pw_ref[g])
        mix_ref[:, lo:hi] = (mixed * ps_ref[:, lo:hi]).astype(BF16)

    conv_b = sw_ref[SCONV_K - 1:SCONV_K, :] * eb_ref[HIST_B:HIST_B + tm, :]
    for j in range(SCONV_K - 1):
        off = HIST_B - (SCONV_K - 1) + j
        conv_b = conv_b + sw_ref[j:j + 1, :] * eb_ref[off:off + tm, :]
    mix_ref[:, p_width:p_width + s_width] = (bg_ref[...] * conv_b).astype(BF16)

    tap0 = HIST_C - (CCONV_K - 1)
    span = tm + HIST_C - SUBLANES
    for c in range(c_width // LANES):
        cols = slice(c * LANES, (c + 1) * LANES)
        for r in range(1, SUBLANES):
            xs_ref[r, 0:span, :] = ec_ref[r:r + span, cols]
        wk = [jnp.broadcast_to(cw_ref[k:k + 1, cols], (SUBLANES, LANES))
              for k in range(CCONV_K)]

        def chunk(i, carry, cols=cols, wk=wk):
            base = pl.multiple_of(i * SUBLANES, SUBLANES)
            acc = [None] * CONV_PARTIAL_SUMS
            for k in range(CCONV_K):
                q, r = divmod(tap0 + k, SUBLANES)
                if r == 0:
                    src = ec_ref[pl.ds(base + q * SUBLANES, SUBLANES), cols]
                else:
                    src = xs_ref[r, pl.ds(base + q * SUBLANES, SUBLANES), :]
                term = wk[k] * src
                a = k % CONV_PARTIAL_SUMS
                acc[a] = term if acc[a] is None else acc[a] + term
            while len(acc) > 1:
                acc = [acc[i] + acc[i + 1] for i in range(0, len(acc), 2)]
            cc_ref[pl.ds(base, SUBLANES), cols] = acc[0]
            return carry

        lax.fori_loop(0, tm // SUBLANES, chunk, 0, unroll=4)

    y_c = _layernorm_silu(cc_ref[...] + cb_ref[...], lg_ref[...], lb_ref[...])
    mix_ref[:, p_width + s_width:] = y_c.astype(BF16)

    xo_ref[...] = x + _dot(mix_ref[...], wout_ref[...])

    @pl.when(t == pl.num_programs(1) - 1)
    def _():
        nsp_ref[0] = ea_ref[HIST_A + tm - POOL_STATE:HIST_A + tm, :]
        nss_ref[0] = eb_ref[HIST_B + tm - (SCONV_K - 1):HIST_B + tm, :]
        nsc_ref[0] = ec_ref[HIST_C + tm - (CCONV_K - 1):HIST_C + tm, :]

    ea_ref[0:HIST_A, :] = ea_ref[tm:tm + HIST_A, :]
    eb_ref[0:HIST_B, :] = eb_ref[tm:tm + HIST_B, :]
    ec_ref[0:HIST_C, :] = ec_ref[tm:tm + HIST_C, :]


def _prompt_mixer(x, n_rows_out, nb, seq, layer, p, win_bf, wout_bf, ffn_w, tm):
    d = x.shape[1]
    d_in = win_bf.shape[1]
    p_width = p["pool_scale"].shape[2]
    s_width = p["sconv_w"].shape[2]
    c_width = p["cconv_w"].shape[2]
    sec = _in_proj_sections(p_width, s_width, c_width)
    n_t = seq // tm
    n_steps = nb * n_t
    lsel3 = lambda b, t: (layer, 0, 0)
    lsel4 = lambda b, t: (layer, 0, 0, 0)
    const2 = lambda b, t: (0, 0)
    step = lambda b, t: b * n_t + t
    n_groups = len(POOL_WINDOWS)
    in_specs = [
        pl.BlockSpec((tm, d), lambda b, t: (step(b, t), 0)),
        _resident((None, 1, d), lsel3),
        _resident((d, d_in), const2),
        _resident((None, n_groups, POOL_GROUP_DIM, POOL_GROUP_DIM), lsel4),
        _resident((None, 1, p_width), lsel3),
        _resident((None, SCONV_K, s_width), lsel3),
        _resident((None, CCONV_K, c_width), lsel3),
        _resident((None, 1, c_width), lsel3),
        _resident((None, 1, c_width), lsel3),
        _resident((None, 1, c_width), lsel3),
        _resident((d, d), const2),
    ]
    out_shape = [
        jax.ShapeDtypeStruct((n_rows_out, d), F32),
        jax.ShapeDtypeStruct((nb, POOL_STATE, p_width), F32),
        jax.ShapeDtypeStruct((nb, SCONV_K - 1, s_width), F32),
        jax.ShapeDtypeStruct((nb, CCONV_K - 1, c_width), F32),
    ]
    out_specs = [
        pl.BlockSpec((tm, d), lambda b, t: (step(b, t), 0)),
        pl.BlockSpec((1, POOL_STATE, p_width), lambda b, t: (b, 0, 0)),
        pl.BlockSpec((1, SCONV_K - 1, s_width), lambda b, t: (b, 0, 0)),
        pl.BlockSpec((1, CCONV_K - 1, c_width), lambda b, t: (b, 0, 0)),
    ]
    for w in ffn_w:
        _, rows, cols = w.shape
        blk = rows // n_steps
        assert blk * n_steps == rows and blk % BF16_ROWS == 0
        in_specs.append(pl.BlockSpec((None, blk, cols), lambda b, t: (layer, step(b, t), 0)))
        out_specs.append(pl.BlockSpec((blk, cols), lambda b, t: (step(b, t), 0)))
        out_shape.append(jax.ShapeDtypeStruct((rows, cols), BF16))
    scratch = [
        pltpu.VMEM((HIST_A + tm, p_width), F32),
        pltpu.VMEM((HIST_B + tm, s_width), F32),
        pltpu.VMEM((HIST_C + tm, c_width), F32),
        pltpu.VMEM((SUBLANES, tm + HIST_C - SUBLANES, LANES), F32),
        pltpu.VMEM((tm, s_width), F32),
        pltpu.VMEM((tm, c_width), F32),
        pltpu.VMEM((tm, d), BF16),
    ]
    return pl.pallas_call(
        functools.partial(_prompt_mixer_kernel, tm=tm, sec=sec),
        grid=(nb, n_t),
        in_specs=in_specs,
        out_specs=out_specs,
        out_shape=out_shape,
        scratch_shapes=scratch,
        compiler_params=pltpu.CompilerParams(
            dimension_semantics=("parallel", "arbitrary"),
            vmem_limit_bytes=VMEM_LIMIT_BYTES),
        name=f"prompt_mixer_l{layer}",
    )(x, p["norm_mix"], win_bf, p["pool_w"], p["pool_scale"], p["sconv_w"],
      p["cconv_w"], p["cconv_b"], p["cconv_ln_g"], p["cconv_ln_b"], wout_bf, *ffn_w)


def _sample_mixer_kernel(x_ref, stp_ref, sts_ref, stc_ref, xbuf_in, nm_ref, win_ref,
                         pw_ref, ps_ref, sw_ref, cw_ref, cb_ref, lg_ref, lb_ref, wout_ref,
                         xo_ref, nu_ref, nv_ref, nglu_ref, mix_ref, *, sec):
    del xbuf_in
    p_width = nu_ref.shape[1]
    s_width = nv_ref.shape[1]
    x = x_ref[...]
    h = _rmsnorm(x, nm_ref[...]).astype(BF16)

    def proj(name):
        lo, hi = sec[name]
        return _dot(h, win_ref[:, lo:hi])

    u = proj("u_a")
    nu_ref[...] = u
    for g, win in enumerate(POOL_WINDOWS):
        lo, hi = g * POOL_GROUP_DIM, (g + 1) * POOL_GROUP_DIM
        s = u[:, lo:hi]
        for j in range(1, win):
            s = s + stp_ref[POOL_STATE - j, :, lo:hi]
        cnt = float(min(PAST_LEN + 1, win))
        pooled = s / cnt - u[:, lo:hi]
        mixed = _dot(pooled.astype(BF16), pw_ref[g])
        mix_ref[:, lo:hi] = (mixed * ps_ref[:, lo:hi]).astype(BF16)

    v = proj("c_gate") * proj("x_b")
    nv_ref[...] = v
    conv_b = sw_ref[SCONV_K - 1:SCONV_K, :] * v
    for j in range(SCONV_K - 1):
        conv_b = conv_b + sw_ref[j:j + 1, :] * sts_ref[j]
    mix_ref[:, p_width:p_width + s_width] = (proj("b_gate") * conv_b).astype(BF16)

    glu = proj("a_c") * _sigmoid(proj("g_c"))
    nglu_ref[...] = glu
    conv_c = cw_ref[CCONV_K - 1:CCONV_K, :] * glu
    for k in range(CCONV_K - 1):
        conv_c = conv_c + cw_ref[k:k + 1, :] * stc_ref[k]
    y_c = _layernorm_silu(conv_c + cb_ref[...], lg_ref[...], lb_ref[...])
    mix_ref[:, p_width + s_width:] = y_c.astype(BF16)

    xo_ref[...] = x + _dot(mix_ref[...], wout_ref[...])


def _sample_mixer(x, x_block, xbuf, states, layer, p, win_bf, wout_bf):
    d = x.shape[1]
    _, k_pool, n, p_width = states[0].shape
    s_width = states[1].shape[3]
    c_width = states[2].shape[3]
    d_in = win_bf.shape[1]
    sec = _in_proj_sections(p_width, s_width, c_width)
    lsel3 = lambda i: (layer, 0, 0)
    lsel4 = lambda i: (layer, 0, 0, 0)
    const2 = lambda i: (0, 0)
    n_groups = len(POOL_WINDOWS)
    assert xbuf.shape[0] % n == 0
    out_block = xbuf.shape[0] // n - 1
    in_specs = [_resident((n, d), lambda i: (x_block, 0))] + [
        _resident((None,) + s.shape[1:], lsel4) for s in states
    ] + [ANY_SPEC] + [
        _resident((None, 1, d), lsel3),
        _resident((d, d_in), const2),
        _resident((None, n_groups, POOL_GROUP_DIM, POOL_GROUP_DIM), lsel4),
        _resident((None, 1, p_width), lsel3),
        _resident((None, SCONV_K, s_width), lsel3),
        _resident((None, CCONV_K, c_width), lsel3),
        _resident((None, 1, c_width), lsel3),
        _resident((None, 1, c_width), lsel3),
        _resident((None, 1, c_width), lsel3),
        _resident((d, d), const2),
    ]
    out_shape = [jax.ShapeDtypeStruct(xbuf.shape, xbuf.dtype)] + [
        jax.ShapeDtypeStruct((n, w), F32) for w in (p_width, s_width, c_width)]
    out_specs = [pl.BlockSpec((n, d), lambda i: (out_block, 0))] + [
        pl.BlockSpec((n, w), const2) for w in (p_width, s_width, c_width)]
    return pl.pallas_call(
        functools.partial(_sample_mixer_kernel, sec=sec),
        grid=(1,),
        in_specs=in_specs,
        out_specs=out_specs,
        out_shape=out_shape,
        scratch_shapes=[pltpu.VMEM((n, d), BF16)],
        input_output_aliases={1 + N_STATES: 0},
        compiler_params=pltpu.CompilerParams(
            dimension_semantics=("arbitrary",),
            vmem_limit_bytes=VMEM_LIMIT_BYTES),
        name=f"sample_mixer_l{layer}",
    )(x, *states, xbuf, p["norm_mix"], win_bf, p["pool_w"], p["pool_scale"],
      p["sconv_w"], p["cconv_w"], p["cconv_b"], p["cconv_ln_g"], p["cconv_ln_b"],
      wout_bf)


def _state_update_kernel(*refs):
    for st_ref, new_ref, out_ref in zip(refs[:N_STATES], refs[N_STATES:2 * N_STATES],
                                        refs[2 * N_STATES:]):
        keep = st_ref.shape[0] - 1
        out_ref[0:keep] = st_ref[1:keep + 1]
        out_ref[keep] = new_ref[...]


def _state_update(states, new_rows, chunk):
    depth, _, n, _ = states[0].shape
    st_specs = [pl.BlockSpec((None, s.shape[1], chunk, s.shape[3]), lambda l, c: (l, 0, c, 0))
                for s in states]
    new_specs = [pl.BlockSpec((None, chunk, r.shape[2]), lambda l, c: (l, c, 0))
                 for r in new_rows]
    return pl.pallas_call(
        _state_update_kernel,
        grid=(depth, n // chunk),
        in_specs=st_specs + new_specs,
        out_specs=st_specs,
        out_shape=[jax.ShapeDtypeStruct(s.shape, s.dtype) for s in states],
        compiler_params=pltpu.CompilerParams(
            dimension_semantics=("parallel", "parallel"),
            vmem_limit_bytes=VMEM_LIMIT_BYTES),
        name="sample_state_update",
    )(*states, *new_rows)


def _ffn_kernel(*refs, apply_final_norm, n_cast):
    x_ref, nf_ref, wg_ref, wu_ref, wd_ref, nfin_ref = refs[:6]
    cast_in = refs[6:6 + n_cast]
    o_ref = refs[6 + n_cast]
    cast_out = refs[7 + n_cast:7 + 2 * n_cast]
    h2_ref = refs[7 + 2 * n_cast]
    j = pl.program_id(1)

    for src, dst in zip(cast_in, cast_out):
        dst[...] = src[...].astype(BF16)

    def hidden_chunk():
        h2 = h2_ref[...]
        gate = _dot(h2, wg_ref[...])
        up = _dot(h2, wu_ref[...])
        f = (gate * _sigmoid(gate) * up).astype(BF16)
        return _dot(f, wd_ref[...])

    @pl.when(j == 0)
    def _():
        x = x_ref[...]
        h2_ref[...] = _rmsnorm(x, nf_ref[...]).astype(BF16)
        o_ref[...] = x + hidden_chunk()

    @pl.when(j > 0)
    def _():
        o_ref[...] += hidden_chunk()

    if apply_final_norm:
        @pl.when(j == pl.num_programs(1) - 1)
        def _():
            o_ref[...] = _rmsnorm(o_ref[...], nfin_ref[...])


def _ffn(x, first_block, n_i, tm, layer, p, wg_bf, wu_bf, wd_bf, tf, apply_final_norm,
         cast_next=()):
    d = x.shape[1]
    f_width = wg_bf.shape[1]
    n_j = f_width // tf
    in_specs = [
        pl.BlockSpec((tm, d), lambda i, j: (first_block + i, 0)),
        pl.BlockSpec((None, 1, d), lambda i, j: (layer, 0, 0)),
        pl.BlockSpec((d, tf), lambda i, j: (0, j)),
        pl.BlockSpec((d, tf), lambda i, j: (0, j)),
        pl.BlockSpec((tf, d), lambda i, j: (j, 0)),
        pl.BlockSpec((1, d), lambda i, j: (0, 0)),
    ]
    out_specs = [pl.BlockSpec((tm, d), lambda i, j: (i, 0))]
    out_shape = [jax.ShapeDtypeStruct((n_i * tm, d), F32)]
    for w in cast_next:
        _, rows, cols = w.shape
        k = _cast_split(rows, n_i, n_j)
        blk = rows // (n_i * k)
        step = lambda i, j, k=k: i * k + jnp.minimum(j, k - 1)
        in_specs.append(pl.BlockSpec((None, blk, cols),
                                     lambda i, j, step=step: (layer + 1, step(i, j), 0)))
        out_specs.append(pl.BlockSpec((blk, cols), lambda i, j, step=step: (step(i, j), 0)))
        out_shape.append(jax.ShapeDtypeStruct((rows, cols), BF16))
    return pl.pallas_call(
        functools.partial(_ffn_kernel, apply_final_norm=apply_final_norm,
                          n_cast=len(cast_next)),
        grid=(n_i, n_j),
        in_specs=in_specs,
        out_specs=out_specs,
        out_shape=out_shape,
        scratch_shapes=[pltpu.VMEM((tm, d), BF16)],
        compiler_params=pltpu.CompilerParams(
            dimension_semantics=("parallel", "arbitrary"),
            vmem_limit_bytes=VMEM_LIMIT_BYTES),
        name=f"ffn_l{layer}_m{n_i * tm}",
    )(x, p["norm_ffn"], wg_bf, wu_bf, wd_bf, p["norm_final"], *cast_next)


def kernel(x_prompt, x_sample, state_pool, state_sconv, state_cconv, norm_mix, w_in,
           pool_w, pool_scale, sconv_w, cconv_w, cconv_b, cconv_ln_g, cconv_ln_b,
           w_out, norm_ffn, w_gate, w_up, w_down, norm_final):
    depth = w_in.shape[0]
    nb, seq, d = x_prompt.shape
    n_dec, dec_seq, _ = x_sample.shape
    assert dec_seq == 1
    n_prompt = nb * seq
    n_rows = n_prompt + n_dec
    assert n_prompt % n_dec == 0
    row = lambda a: a[:, None, :]
    p = dict(
        norm_mix=row(norm_mix), pool_w=pool_w.astype(BF16), pool_scale=row(pool_scale),
        sconv_w=sconv_w, cconv_w=cconv_w, cconv_b=row(cconv_b),
        cconv_ln_g=row(cconv_ln_g), cconv_ln_b=row(cconv_ln_b), norm_ffn=row(norm_ffn),
        norm_final=norm_final[None, :])

    mixer_tm, ffn_tiles, ffn_tf = _tiles(n_prompt, n_dec, seq)

    win_bf, wout_bf = w_in[0].astype(BF16), w_out[0].astype(BF16)

    hist_major = lambda s: jnp.transpose(s, (0, 2, 1, 3))
    states = tuple(hist_major(s) for s in (state_pool, state_sconv, state_cconv))

    x_p = x_prompt.reshape(n_prompt, d)
    x_s, x_s_block = x_sample.reshape(n_dec, d), 0
    prompt_states = ([], [], [])
    sample_rows = ([], [], [])
    for l in range(depth):
        last = l == depth - 1
        xbuf, nsp, nss, nsc, wg_bf, wu_bf, wd_bf = _prompt_mixer(
            x_p, n_rows, nb, seq, l, p, win_bf, wout_bf, (w_gate, w_up, w_down), mixer_tm)
        for acc, s in zip(prompt_states, (nsp, nss, nsc)):
            acc.append(s)
        xbuf, *new_rows = _sample_mixer(x_s, x_s_block, xbuf, states, l, p, win_bf, wout_bf)
        for acc, r in zip(sample_rows, new_rows):
            acc.append(r)
        if not last:
            outs = _ffn(xbuf, 0, ffn_tiles, n_rows // ffn_tiles, l, p, wg_bf, wu_bf, wd_bf,
                        ffn_tf, apply_final_norm=False, cast_next=(w_in, w_out))
            x_p = x_s = outs[0]
            x_s_block = n_prompt // n_dec
            win_bf, wout_bf = outs[1:]
        else:
            y_p = _ffn(xbuf, 0, ffn_tiles, n_prompt // ffn_tiles, l, p, wg_bf, wu_bf, wd_bf,
                       ffn_tf, apply_final_norm=True)[0]
            y_s = _ffn(xbuf, n_prompt // n_dec, 1, n_dec, l, p, wg_bf, wu_bf, wd_bf,
                       ffn_tf, apply_final_norm=True)[0]

    sp_p, ss_p, sc_p = (jnp.stack(s) for s in prompt_states)
    new_states = _state_update(states, [jnp.stack(r) for r in sample_rows],
                               chunk=min(32, n_dec))
    sp_s, ss_s, sc_s = (hist_major(s) for s in new_states)
    return (y_p.reshape(nb, seq, d), y_s.reshape(n_dec, dec_seq, d),
            sp_p, sp_s, ss_p, ss_s, sc_p, sc_s)
```

```python
import functools

import jax
import jax.numpy as jnp
from jax import lax
from jax.experimental import pallas as pl
from jax.experimental.pallas import tpu as pltpu

EPS = 1e-6
PAST_LEN = 16384
POOL_WINDOWS = (2, 4, 8, 16)
POOL_GROUP_DIM = 128
POOL_STATE = 15
SCONV_K = 3
CCONV_K = 31

SUBLANES = 8
LANES = 128
BF16_ROWS = 16
HIST_A = 16
HIST_B = 16
HIST_C = 32
CONV_PARTIAL_SUMS = 4
VMEM_LIMIT_BYTES = 58 * 1024 * 1024
N_STATES = 3

BF16 = jnp.bfloat16
F32 = jnp.float32
ANY_SPEC = pl.BlockSpec(memory_space=pl.ANY)


def _tiles(n_prompt_rows, n_sample_rows, seq):
    mixer_tm = min(256, seq)
    ffn_row_tiles = max(1, n_prompt_rows // 1024)
    assert n_prompt_rows % ffn_row_tiles == 0 and n_sample_rows % ffn_row_tiles == 0
    assert ((n_prompt_rows + n_sample_rows) // ffn_row_tiles) % BF16_ROWS == 0
    ffn_tf = 512
    return mixer_tm, ffn_row_tiles, ffn_tf


def _cast_split(n_rows, n_outer, n_inner):
    for k in range(n_inner, 0, -1):
        if n_rows % (n_outer * k) == 0 and (n_rows // (n_outer * k)) % BF16_ROWS == 0:
            return k
    raise ValueError(f"cannot split {n_rows} rows over {n_outer}x{n_inner} grid steps")


def _rmsnorm(x, g):
    ms = jnp.mean(x * x, axis=-1, keepdims=True)
    return x * lax.rsqrt(ms + EPS) * g


def _dot(a, b):
    return jnp.dot(a, b, preferred_element_type=F32)


def _sigmoid(x):
    return 0.5 * jnp.tanh(0.5 * x) + 0.5


def _layernorm_silu(c, g, b):
    mu = jnp.mean(c, axis=-1, keepdims=True)
    d = c - mu
    var = jnp.mean(d * d, axis=-1, keepdims=True)
    y = d * lax.rsqrt(var + EPS) * g + b
    return y * _sigmoid(y)


def _in_proj_sections(p_width, s_width, c_width):
    o = [0, p_width]
    for w in (s_width, s_width, s_width, c_width, c_width):
        o.append(o[-1] + w)
    names = ("u_a", "b_gate", "c_gate", "x_b", "a_c", "g_c")
    return {n: (o[i], o[i + 1]) for i, n in enumerate(names)}


def _resident(block_shape, index_map):
    return pl.BlockSpec(block_shape, index_map, pipeline_mode=pl.Buffered(1))


def _prompt_mixer_kernel(x_ref, nm_ref, win_ref, pw_ref, ps_ref, sw_ref, cw_ref, cb_ref,
                         lg_ref, lb_ref, wout_ref, wg_ref, wu_ref, wd_ref,
                         xo_ref, nsp_ref, nss_ref, nsc_ref, wgo_ref, wuo_ref, wdo_ref,
                         ea_ref, eb_ref, ec_ref, xs_ref, bg_ref, cc_ref, mix_ref,
                         *, tm, sec):
    t = pl.program_id(1)
    p_width = ea_ref.shape[1]
    s_width = eb_ref.shape[1]
    c_width = ec_ref.shape[1]

    @pl.when(t == 0)
    def _():
        ea_ref[0:HIST_A, :] = jnp.zeros((HIST_A, p_width), F32)
        eb_ref[0:HIST_B, :] = jnp.zeros((HIST_B, s_width), F32)
        ec_ref[0:HIST_C, :] = jnp.zeros((HIST_C, c_width), F32)

    h = _rmsnorm(x_ref[...], nm_ref[...]).astype(BF16)

    def proj(name):
        lo, hi = sec[name]
        return _dot(h, win_ref[:, lo:hi])

    ea_ref[HIST_A:HIST_A + tm, :] = proj("u_a")
    wgo_ref[...] = wg_ref[...].astype(BF16)
    bg_ref[...] = proj("b_gate")
    wuo_ref[...] = wu_ref[...].astype(BF16)
    eb_ref[HIST_B:HIST_B + tm, :] = proj("c_gate") * proj("x_b")
    wdo_ref[...] = wd_ref[...].astype(BF16)
    ec_ref[HIST_C:HIST_C + tm, :] = proj("a_c") * _sigmoid(proj("g_c"))

    pos = t * tm + lax.broadcasted_iota(jnp.int32, (tm, POOL_GROUP_DIM), 0)
    for g, win in enumerate(POOL_WINDOWS):
        lo, hi = g * POOL_GROUP_DIM, (g + 1) * POOL_GROUP_DIM
        u = ea_ref[HIST_A:HIST_A + tm, lo:hi]
        s = u
        for j in range(1, win):
            s = s + ea_ref[HIST_A - j:HIST_A - j + tm, lo:hi]
        cnt = jnp.minimum(pos + 1, win).astype(F32)
        pooled = s / cnt - u
        mixed = _dot(pooled.astype(BF16), pw_ref[g])
        mix_ref[:, lo:hi] = (mixed * ps_ref[:, lo:hi]).astype(BF16)

    conv_b = sw_ref[SCONV_K - 1:SCONV_K, :] * eb_ref[HIST_B:HIST_B + tm, :]
    for j in range(SCONV_K - 1):
        off = HIST_B - (SCONV_K - 1) + j
        conv_b = conv_b + sw_ref[j:j + 1, :] * eb_ref[off:off + tm, :]
    mix_ref[:, p_width:p_width + s_width] = (bg_ref[...] * conv_b).astype(BF16)

    tap0 = HIST_C - (CCONV_K - 1)
    span = tm + HIST_C - SUBLANES
    for c in range(c_width // LANES):
        cols = slice(c * LANES, (c + 1) * LANES)
        for r in range(1, SUBLANES):
            xs_ref[r, 0:span, :] = ec_ref[r:r + span, cols]
        wk = [jnp.broadcast_to(cw_ref[k:k + 1, cols], (SUBLANES, LANES))
              for k in range(CCONV_K)]

        def chunk(i, carry, cols=cols, wk=wk):
            base = pl.multiple_of(i * SUBLANES, SUBLANES)
            acc = [None] * CONV_PARTIAL_SUMS
            for k in range(CCONV_K):
                q, r = divmod(tap0 + k, SUBLANES)
                if r == 0:
                    src = ec_ref[pl.ds(base + q * SUBLANES, SUBLANES), cols]
                else:
                    src = xs_ref[r, pl.ds(base + q * SUBLANES, SUBLANES), :]
                term = wk[k] * src
                a = k % CONV_PARTIAL_SUMS
                acc[a] = term if acc[a] is None else acc[a] + term
            while len(acc) > 1:
                acc = [acc[i] + acc[i + 1] for i in range(0, len(acc), 2)]
            cc_ref[pl.ds(base, SUBLANES), cols] = acc[0]
            return carry

        lax.fori_loop(0, tm // SUBLANES, chunk, 0, unroll=4)

    y_c = _layernorm_silu(cc_ref[...] + cb_ref[...], lg_ref[...], lb_ref[...])
    mix_ref[:, p_width + s_width:] = y_c.astype(BF16)

    xo_ref[...] = x_ref[...] + _dot(mix_ref[...], wout_ref[...])

    @pl.when(t == pl.num_programs(1) - 1)
    def _():
        nsp_ref[0] = ea_ref[HIST_A + tm - POOL_STATE:HIST_A + tm, :]
        nss_ref[0] = eb_ref[HIST_B + tm - (SCONV_K - 1):HIST_B + tm, :]
        nsc_ref[0] = ec_ref[HIST_C + tm - (CCONV_K - 1):HIST_C + tm, :]

    ea_ref[0:HIST_A, :] = ea_ref[tm:tm + HIST_A, :]
    eb_ref[0:HIST_B, :] = eb_ref[tm:tm + HIST_B, :]
    ec_ref[0:HIST_C, :] = ec_ref[tm:tm + HIST_C, :]


def _prompt_mixer(x, n_rows_out, nb, seq, layer, p, win_bf, wout_bf, ffn_w, tm):
    d = x.shape[1]
    d_in = win_bf.shape[1]
    p_width = p["pool_scale"].shape[2]
    s_width = p["sconv_w"].shape[2]
    c_width = p["cconv_w"].shape[2]
    sec = _in_proj_sections(p_width, s_width, c_width)
    n_t = seq // tm
    n_steps = nb * n_t
    lsel3 = lambda b, t: (layer, 0, 0)
    lsel4 = lambda b, t: (layer, 0, 0, 0)
    const2 = lambda b, t: (0, 0)
    step = lambda b, t: b * n_t + t
    n_groups = len(POOL_WINDOWS)
    in_specs = [
        pl.BlockSpec((tm, d), lambda b, t: (step(b, t), 0)),
        _resident((None, 1, d), lsel3),
        _resident((d, d_in), const2),
        _resident((None, n_groups, POOL_GROUP_DIM, POOL_GROUP_DIM), lsel4),
        _resident((None, 1, p_width), lsel3),
        _resident((None, SCONV_K, s_width), lsel3),
        _resident((None, CCONV_K, c_width), lsel3),
        _resident((None, 1, c_width), lsel3),
        _resident((None, 1, c_width), lsel3),
        _resident((None, 1, c_width), lsel3),
        _resident((d, d), const2),
    ]
    out_shape = [
        jax.ShapeDtypeStruct((n_rows_out, d), F32),
        jax.ShapeDtypeStruct((nb, POOL_STATE, p_width), F32),
        jax.ShapeDtypeStruct((nb, SCONV_K - 1, s_width), F32),
        jax.ShapeDtypeStruct((nb, CCONV_K - 1, c_width), F32),
    ]
    out_specs = [
        pl.BlockSpec((tm, d), lambda b, t: (step(b, t), 0)),
        pl.BlockSpec((1, POOL_STATE, p_width), lambda b, t: (b, 0, 0)),
        pl.BlockSpec((1, SCONV_K - 1, s_width), lambda b, t: (b, 0, 0)),
        pl.BlockSpec((1, CCONV_K - 1, c_width), lambda b, t: (b, 0, 0)),
    ]
    for w in ffn_w:
        _, rows, cols = w.shape
        blk = rows // n_steps
        assert blk * n_steps == rows and blk % BF16_ROWS == 0
        in_specs.append(pl.BlockSpec((None, blk, cols), lambda b, t: (layer, step(b, t), 0)))
        out_specs.append(pl.BlockSpec((blk, cols), lambda b, t: (step(b, t), 0)))
        out_shape.append(jax.ShapeDtypeStruct((rows, cols), BF16))
    scratch = [
        pltpu.VMEM((HIST_A + tm, p_width), F32),
        pltpu.VMEM((HIST_B + tm, s_width), F32),
        pltpu.VMEM((HIST_C + tm, c_width), F32),
        pltpu.VMEM((SUBLANES, tm + HIST_C - SUBLANES, LANES), F32),
        pltpu.VMEM((tm, s_width), F32),
        pltpu.VMEM((tm, c_width), F32),
        pltpu.VMEM((tm, d), BF16),
    ]
    return pl.pallas_call(
        functools.partial(_prompt_mixer_kernel, tm=tm, sec=sec),
        grid=(nb, n_t),
        in_specs=in_specs,
        out_specs=out_specs,
        out_shape=out_shape,
        scratch_shapes=scratch,
        compiler_params=pltpu.CompilerParams(
            dimension_semantics=("parallel", "arbitrary"),
            vmem_limit_bytes=VMEM_LIMIT_BYTES),
        name=f"prompt_mixer_l{layer}",
    )(x, p["norm_mix"], win_bf, p["pool_w"], p["pool_scale"], p["sconv_w"],
      p["cconv_w"], p["cconv_b"], p["cconv_ln_g"], p["cconv_ln_b"], wout_bf, *ffn_w)


def _sample_mixer_kernel(x_ref, stp_ref, sts_ref, stc_ref, xbuf_in, nm_ref, win_ref,
                         pw_ref, ps_ref, sw_ref, cw_ref, cb_ref, lg_ref, lb_ref, wout_ref,
                         xo_ref, nu_ref, nv_ref, nglu_ref, mix_ref, *, sec):
    del xbuf_in
    p_width = nu_ref.shape[1]
    s_width = nv_ref.shape[1]
    x = x_ref[...]
    h = _rmsnorm(x, nm_ref[...]).astype(BF16)

    def proj(name):
        lo, hi = sec[name]
        return _dot(h, win_ref[:, lo:hi])

    u = proj("u_a")
    nu_ref[...] = u
    for g, win in enumerate(POOL_WINDOWS):
        lo, hi = g * POOL_GROUP_DIM, (g + 1) * POOL_GROUP_DIM
        s = u[:, lo:hi]
        for j in range(1, win):
            s = s + stp_ref[POOL_STATE - j, :, lo:hi]
        cnt = float(min(PAST_LEN + 1, win))
        pooled = s / cnt - u[:, lo:hi]
        mixed = _dot(pooled.astype(BF16), pw_ref[g])
        mix_ref[:, lo:hi] = (mixed * ps_ref[:, lo:hi]).astype(BF16)

    v = proj("c_gate") * proj("x_b")
    nv_ref[...] = v
    conv_b = sw_ref[SCONV_K - 1:SCONV_K, :] * v
    for j in range(SCONV_K - 1):
        conv_b = conv_b + sw_ref[j:j + 1, :] * sts_ref[j]
    mix_ref[:, p_width:p_width + s_width] = (proj("b_gate") * conv_b).astype(BF16)

    glu = proj("a_c") * _sigmoid(proj("g_c"))
    nglu_ref[...] = glu
    conv_c = cw_ref[CCONV_K - 1:CCONV_K, :] * glu
    for k in range(CCONV_K - 1):
        conv_c = conv_c + cw_ref[k:k + 1, :] * stc_ref[k]
    y_c = _layernorm_silu(conv_c + cb_ref[...], lg_ref[...], lb_ref[...])
    mix_ref[:, p_width + s_width:] = y_c.astype(BF16)

    xo_ref[...] = x + _dot(mix_ref[...], wout_ref[...])


def _sample_mixer(x, x_block, xbuf, states, layer, p, win_bf, wout_bf):
    d = x.shape[1]
    _, k_pool, n, p_width = states[0].shape
    s_width = states[1].shape[3]
    c_width = states[2].shape[3]
    d_in = win_bf.shape[1]
    sec = _in_proj_sections(p_width, s_width, c_width)
    lsel3 = lambda i: (layer, 0, 0)
    lsel4 = lambda i: (layer, 0, 0, 0)
    const2 = lambda i: (0, 0)
    n_groups = len(POOL_WINDOWS)
    assert xbuf.shape[0] % n == 0
    out_block = xbuf.shape[0] // n - 1
    in_specs = [_resident((n, d), lambda i: (x_block, 0))] + [
        _resident((None,) + s.shape[1:], lsel4) for s in states
    ] + [ANY_SPEC] + [
        _resident((None, 1, d), lsel3),
        _resident((d, d_in), const2),
        _resident((None, n_groups, POOL_GROUP_DIM, POOL_GROUP_DIM), lsel4),
        _resident((None, 1, p_width), lsel3),
        _resident((None, SCONV_K, s_width), lsel3),
        _resident((None, CCONV_K, c_width), lsel3),
        _resident((None, 1, c_width), lsel3),
        _resident((None, 1, c_width), lsel3),
        _resident((None, 1, c_width), lsel3),
        _resident((d, d), const2),
    ]
    out_shape = [jax.ShapeDtypeStruct(xbuf.shape, xbuf.dtype)] + [
        jax.ShapeDtypeStruct((n, w), F32) for w in (p_width, s_width, c_width)]
    out_specs = [pl.BlockSpec((n, d), lambda i: (out_block, 0))] + [
        pl.BlockSpec((n, w), const2) for w in (p_width, s_width, c_width)]
    return pl.pallas_call(
        functools.partial(_sample_mixer_kernel, sec=sec),
        grid=(1,),
        in_specs=in_specs,
        out_specs=out_specs,
        out_shape=out_shape,
        scratch_shapes=[pltpu.VMEM((n, d), BF16)],
        input_output_aliases={1 + N_STATES: 0},
        compiler_params=pltpu.CompilerParams(
            dimension_semantics=("arbitrary",),
            vmem_limit_bytes=VMEM_LIMIT_BYTES),
        name=f"sample_mixer_l{layer}",
    )(x, *states, xbuf, p["norm_mix"], win_bf, p["pool_w"], p["pool_scale"],
      p["sconv_w"], p["cconv_w"], p["cconv_b"], p["cconv_ln_g"], p["cconv_ln_b"],
      wout_bf)


def _ffn_kernel(*refs, apply_final_norm, n_cast, n_shift):
    x_ref, nf_ref, wg_ref, wu_ref, wd_ref, nfin_ref = refs[:6]
    n_in = 6 + n_cast + 2 * n_shift
    cast_in = refs[6:6 + n_cast]
    shift_in = refs[6 + n_cast:6 + n_cast + n_shift]
    shift_new = refs[6 + n_cast + n_shift:n_in]
    o_ref = refs[n_in]
    cast_out = refs[n_in + 1:n_in + 1 + n_cast]
    shift_out = refs[n_in + 1 + n_cast:n_in + 1 + n_cast + n_shift]
    h2_ref = refs[n_in + 1 + n_cast + n_shift]
    j = pl.program_id(1)

    def side_work():
        for src, dst in zip(cast_in, cast_out):
            dst[...] = src[...].astype(BF16)
        for st_ref, new_ref, out_ref in zip(shift_in, shift_new, shift_out):
            keep = st_ref.shape[0] - 1
            out_ref[0:keep] = st_ref[1:keep + 1]
            out_ref[keep] = new_ref[...]

    def hidden_chunk():
        h2 = h2_ref[...]
        gate = _dot(h2, wg_ref[...])
        up = _dot(h2, wu_ref[...])
        side_work()
        f = (gate * _sigmoid(gate) * up).astype(BF16)
        return _dot(f, wd_ref[...])

    @pl.when(j == 0)
    def _():
        h2_ref[...] = _rmsnorm(x_ref[...], nf_ref[...]).astype(BF16)
        o_ref[...] = x_ref[...] + hidden_chunk()

    @pl.when(j > 0)
    def _():
        o_ref[...] += hidden_chunk()

    if apply_final_norm:
        @pl.when(j == pl.num_programs(1) - 1)
        def _():
            o_ref[...] = _rmsnorm(o_ref[...], nfin_ref[...])


def _ffn(x, first_block, n_i, tm, layer, p, wg_bf, wu_bf, wd_bf, tf, apply_final_norm,
         cast_next=(), shift_states=(), shift_rows=(), shift_chunk=SUBLANES):
    d = x.shape[1]
    f_width = wg_bf.shape[1]
    n_j = f_width // tf
    in_specs = [
        pl.BlockSpec((tm, d), lambda i, j: (first_block + i, 0)),
        pl.BlockSpec((None, 1, d), lambda i, j: (layer, 0, 0)),
        pl.BlockSpec((d, tf), lambda i, j: (0, j)),
        pl.BlockSpec((d, tf), lambda i, j: (0, j)),
        pl.BlockSpec((tf, d), lambda i, j: (j, 0)),
        pl.BlockSpec((1, d), lambda i, j: (0, 0)),
    ]
    out_specs = [pl.BlockSpec((tm, d), lambda i, j: (i, 0))]
    out_shape = [jax.ShapeDtypeStruct((n_i * tm, d), F32)]
    for w in cast_next:
        _, rows, cols = w.shape
        k = _cast_split(rows, n_i, n_j)
        blk = rows // (n_i * k)
        step = lambda i, j, k=k: i * k + jnp.minimum(j, k - 1)
        in_specs.append(pl.BlockSpec((None, blk, cols),
                                     lambda i, j, step=step: (layer + 1, step(i, j), 0)))
        out_specs.append(pl.BlockSpec((blk, cols), lambda i, j, step=step: (step(i, j), 0)))
        out_shape.append(jax.ShapeDtypeStruct((rows, cols), BF16))
    if shift_states:
        depth, _, n, _ = shift_states[0].shape
        n_chunks = n // shift_chunk
        assert n_chunks * shift_chunk == n and depth * n_chunks <= n_i * n_j
        blk = lambda i, j: jnp.minimum(i * n_j + j, depth * n_chunks - 1)
        st_specs = [pl.BlockSpec((None, s.shape[1], shift_chunk, s.shape[3]),
                                 lambda i, j: (blk(i, j) // n_chunks, 0, blk(i, j) % n_chunks, 0))
                    for s in shift_states]
        in_specs += st_specs
        in_specs += [pl.BlockSpec((None, shift_chunk, r.shape[2]),
                                  lambda i, j: (blk(i, j) // n_chunks, blk(i, j) % n_chunks, 0))
                     for r in shift_rows]
        out_specs += st_specs
        out_shape += [jax.ShapeDtypeStruct(s.shape, s.dtype) for s in shift_states]
    return pl.pallas_call(
        functools.partial(_ffn_kernel, apply_final_norm=apply_final_norm,
                          n_cast=len(cast_next), n_shift=len(shift_states)),
        grid=(n_i, n_j),
        in_specs=in_specs,
        out_specs=out_specs,
        out_shape=out_shape,
        scratch_shapes=[pltpu.VMEM((tm, d), BF16)],
        compiler_params=pltpu.CompilerParams(
            dimension_semantics=("parallel", "arbitrary"),
            vmem_limit_bytes=VMEM_LIMIT_BYTES),
        name=f"ffn_l{layer}_m{n_i * tm}",
    )(x, p["norm_ffn"], wg_bf, wu_bf, wd_bf, p["norm_final"], *cast_next,
      *shift_states, *shift_rows)


def kernel(x_prompt, x_sample, state_pool, state_sconv, state_cconv, norm_mix, w_in,
           pool_w, pool_scale, sconv_w, cconv_w, cconv_b, cconv_ln_g, cconv_ln_b,
           w_out, norm_ffn, w_gate, w_up, w_down, norm_final):
    depth = w_in.shape[0]
    nb, seq, d = x_prompt.shape
    n_dec, dec_seq, _ = x_sample.shape
    assert dec_seq == 1
    n_prompt = nb * seq
    n_rows = n_prompt + n_dec
    assert n_prompt % n_dec == 0
    row = lambda a: a[:, None, :]
    p = dict(
        norm_mix=row(norm_mix), pool_w=pool_w.astype(BF16), pool_scale=row(pool_scale),
        sconv_w=sconv_w, cconv_w=cconv_w, cconv_b=row(cconv_b),
        cconv_ln_g=row(cconv_ln_g), cconv_ln_b=row(cconv_ln_b), norm_ffn=row(norm_ffn),
        norm_final=norm_final[None, :])

    mixer_tm, ffn_tiles, ffn_tf = _tiles(n_prompt, n_dec, seq)

    win_bf, wout_bf = w_in[0].astype(BF16), w_out[0].astype(BF16)

    hist_major = lambda s: jnp.transpose(s, (0, 2, 1, 3))
    states = tuple(hist_major(s) for s in (state_pool, state_sconv, state_cconv))

    x_p = x_prompt.reshape(n_prompt, d)
    x_s, x_s_block = x_sample.reshape(n_dec, d), 0
    prompt_states = ([], [], [])
    sample_rows = ([], [], [])
    for l in range(depth):
        last = l == depth - 1
        xbuf, nsp, nss, nsc, wg_bf, wu_bf, wd_bf = _prompt_mixer(
            x_p, n_rows, nb, seq, l, p, win_bf, wout_bf, (w_gate, w_up, w_down), mixer_tm)
        for acc, s in zip(prompt_states, (nsp, nss, nsc)):
            acc.append(s)
        xbuf, *new_rows = _sample_mixer(x_s, x_s_block, xbuf, states, l, p, win_bf, wout_bf)
        for acc, r in zip(sample_rows, new_rows):
            acc.append(r)
        if not last:
            outs = _ffn(xbuf, 0, ffn_tiles, n_rows // ffn_tiles, l, p, wg_bf, wu_bf, wd_bf,
                        ffn_tf, apply_final_norm=False, cast_next=(w_in, w_out))
            x_p = x_s = outs[0]
            x_s_block = n_prompt // n_dec
            win_bf, wout_bf = outs[1:]
        else:
            y_p, *new_states = _ffn(
                xbuf, 0, ffn_tiles, n_prompt // ffn_tiles, l, p, wg_bf, wu_bf, wd_bf,
                ffn_tf, apply_final_norm=True, shift_states=states,
                shift_rows=[jnp.stack(r) for r in sample_rows])
            y_s = _ffn(xbuf, n_prompt // n_dec, 1, n_dec, l, p, wg_bf, wu_bf, wd_bf,
                       ffn_tf, apply_final_norm=True)[0]

    sp_p, ss_p, sc_p = (jnp.stack(s) for s in prompt_states)
    sp_s, ss_s, sc_s = (hist_major(s) for s in new_states)
    return (y_p.reshape(nb, seq, d), y_s.reshape(n_dec, dec_seq, d),
            sp_p, sp_s, ss_p, ss_s, sc_p, sc_s)
```
